```python
import math
import jax
import jax.numpy as jnp
from jax import lax
import numpy as np

D_MODEL = 1024
BATCH = 16
SEQ = 256
DEPTH = 2
DEC_BATCH = 4
DEC_SEQ = 4096
PAST_LEN = 256

GRID_W = 64
HY_WIDTH = D_MODEL // 4
HEAD_DIM = 64
N_HEADS = (D_MODEL // 2) // HEAD_DIM
N_KV_HEADS = N_HEADS // 4
GQ = N_HEADS // N_KV_HEADS
ATT_WIDTH = N_HEADS * HEAD_DIM
KV_WIDTH = N_KV_HEADS * HEAD_DIM
S5_WIDTH = D_MODEL // 4
S5_GROUP = 16
S5_GROUPS = S5_WIDTH // S5_GROUP
S5_STATE = 64
N_BRANCH = 3
IN_WIDTH = 3 * HY_WIDTH + ATT_WIDTH + 2 * KV_WIDTH + S5_WIDTH + N_BRANCH * D_MODEL
HY_ORDER = 2
HY_BANDS = 16
HY_EMB = 1 + 2 * HY_BANDS
HY_HIDDEN = 64
WINDOW = 128
Q_BLOCK = 128
ROPE_BASE = 10000.0
NEG_INF = -1e30
N_EXPERTS = 32
TOP_K = 4
D_EXPERT = D_MODEL
SWIGLU_LIMIT = 7.0
SWIGLU_ALPHA = 1.702
MOE_BLOCK = 128
RMS_EPS = 1e-6

kernel_name = "hybrid_dit_hyena_swa_s5_moe_step"


def _rmsnorm(x, g):
    xf = x.astype(jnp.float32)
    y = xf * lax.rsqrt(jnp.mean(xf * xf, axis=-1, keepdims=True) + RMS_EPS)
    return (y * g.astype(jnp.float32)).astype(x.dtype)


def _modulation(cond, w_mod, b_mod):
    m = jax.nn.silu(cond) @ w_mod + b_mod
    return [t[:, None, :] for t in jnp.split(m, 6, axis=-1)]


def _short_conv(x, w, b):
    xp = jnp.pad(x, ((0, 0), (1, 1), (0, 0)))
    return xp[:, :-2] * w[0] + xp[:, 1:-1] * w[1] + xp[:, 2:] * w[2] + b


def _hyena_filters(L, f_w1, f_b1, f_freq, f_w2, f_b2, f_w3, decay):
    t = jnp.arange(L, dtype=jnp.float32)
    t_norm = t / max(L - 1, 1)
    bands = jnp.arange(1, HY_BANDS + 1, dtype=jnp.float32)
    ang = (2.0 * math.pi / L) * t[:, None] * bands[None, :]
    z = jnp.concatenate([t_norm[:, None], jnp.cos(ang), jnp.sin(ang)], axis=-1)
    h = jnp.sin(f_freq[0] * (z @ f_w1 + f_b1))
    h = jnp.sin(f_freq[1] * (h @ f_w2 + f_b2))
    h = (h @ f_w3).astype(jnp.float32).reshape(L, HY_ORDER, 2, HY_WIDTH)
    h = h * jnp.exp(-t_norm[:, None, None, None] * jnp.abs(decay.astype(jnp.float32))[None])
    return h / (jnp.sum(jnp.abs(h), axis=(0, 2), keepdims=True) + 1e-6)


def _bidir_fftconv(u, h_fwd, h_bwd, bias):
    L, C = u.shape[1], u.shape[2]
    k = jnp.concatenate([h_fwd, jnp.zeros((1, C), jnp.float32), h_bwd[:0:-1]], axis=0)
    uf = jnp.fft.rfft(u.astype(jnp.float32), n=2 * L, axis=1)
    kf = jnp.fft.rfft(k, n=2 * L, axis=0)
    y = jnp.fft.irfft(uf * kf[None], n=2 * L, axis=1)[:, :L]
    return (y + u.astype(jnp.float32) * bias.astype(jnp.float32)).astype(u.dtype)


def _hyena(proj_h, p):
    L = proj_h.shape[1]
    uc = _short_conv(proj_h, p["hy_short_w"], p["hy_short_b"])
    x1, x2, v = jnp.split(uc, 3, axis=-1)
    h = _hyena_filters(L, p["hy_f_w1"], p["hy_f_b1"], p["hy_f_freq"], p["hy_f_w2"],
                       p["hy_f_b2"], p["hy_f_w3"], p["hy_decay"])
    z = x1 * _bidir_fftconv(v, h[:, 0, 0], h[:, 0, 1], p["hy_bias"][0])
    z = x2 * _bidir_fftconv(z, h[:, 1, 0], h[:, 1, 1], p["hy_bias"][1])
    return z


def _axial_rope(x):
    L = x.shape[1]
    rows = L // GRID_W
    t = jnp.arange(rows * GRID_W)
    row = (t // GRID_W).astype(jnp.float32)
    col = (t % GRID_W).astype(jnp.float32)
    half = HEAD_DIM // 2
    quarter = half // 2
    inv = ROPE_BASE ** (-jnp.arange(quarter, dtype=jnp.float32) / quarter)

    def rot(xa, pos):
        ang = pos[:, None] * inv[None, :]
        cos = jnp.cos(ang)[None, :, None, :]
        sin = jnp.sin(ang)[None, :, None, :]
        a, b = xa[..., :quarter], xa[..., quarter:]
        return jnp.concatenate([a * cos - b * sin, b * cos + a * sin], axis=-1)

    xf = x.astype(jnp.float32)
    return jnp.concatenate([rot(xf[..., :half], row), rot(xf[..., half:], col)], axis=-1).astype(x.dtype)


def _sink_attend(q, keys, values, masks, sink):
    scale = HEAD_DIM ** -0.5
    logits = []
    for k_i, m_i in zip(keys, masks):
        s = jnp.einsum("bqhgd,bkhd->bhgqk", q, k_i, preferred_element_type=jnp.float32) * scale
        if m_i is not None:
            s = jnp.where(m_i, s, NEG_INF)
        logits.append(s)
    b, qb = q.shape[0], q.shape[1]
    sink_col = jnp.broadcast_to(sink.astype(jnp.float32).reshape(1, N_KV_HEADS, GQ, 1, 1),
                                (b, N_KV_HEADS, GQ, qb, 1))
    probs = jax.nn.softmax(jnp.concatenate(logits + [sink_col], axis=-1), axis=-1)
    out = 0.0
    start = 0
    for v_i in values:
        n = v_i.shape[1]
        out = out + jnp.einsum("bhgqk,bkhd->bqhgd", probs[..., start:start + n].astype(v_i.dtype), v_i)
        start += n
    return out


def _context_attention(q, k, v, sink):
    b, L = q.shape[0], q.shape[1]
    nb = L // Q_BLOCK
    qb = q.reshape(b, nb, Q_BLOCK, N_KV_HEADS, GQ, HEAD_DIM).transpose(1, 0, 2, 3, 4, 5)
    out = lax.map(lambda qi: _sink_attend(qi, (k,), (v,), (None,), sink), qb)
    return out.transpose(1, 0, 2, 3, 4, 5).reshape(b, L, ATT_WIDTH)


def _latent_attention(q, k, v, ck, cv, sink):
    b, L = q.shape[0], q.shape[1]
    nb = L // Q_BLOCK
    qb = q.reshape(b, nb, Q_BLOCK, N_KV_HEADS, GQ, HEAD_DIM).transpose(1, 0, 2, 3, 4, 5)

    def windows(t):
        tp = jnp.pad(t, ((0, 0), (Q_BLOCK, Q_BLOCK), (0, 0), (0, 0)))
        tp = tp.reshape(b, nb + 2, Q_BLOCK, N_KV_HEADS, HEAD_DIM)
        w = jnp.concatenate([tp[:, :-2], tp[:, 1:-1], tp[:, 2:]], axis=2)
        return w.transpose(1, 0, 2, 3, 4)

    kw, vw = windows(k), windows(v)
    blk = jnp.arange(nb)[:, None, None]
    qpos = blk * Q_BLOCK + jnp.arange(Q_BLOCK)[None, :, None]
    kpos = (blk - 1) * Q_BLOCK + jnp.arange(3 * Q_BLOCK)[None, None, :]
    band = (jnp.abs(qpos - kpos) <= WINDOW) & (kpos >= 0) & (kpos < L)

    def attend(args):
        qi, ki, vi, mi = args
        return _sink_attend(qi, (ki, ck), (vi, cv), (mi, None), sink)

    out = lax.map(attend, (qb, kw, vw, band))
    return out.transpose(1, 0, 2, 3, 4, 5).reshape(b, L, ATT_WIDTH)


def _ssm_combine(e1, e2):
    a1, b1 = e1
    a2, b2 = e2
    return a1 * a2, a2 * b1 + b2


def _s5_direction(u, p, d, h0, reverse):
    lam = lax.complex(jnp.minimum(p["s5_a_re"][d].astype(jnp.float32), -1e-4),
                      p["s5_a_im"][d].astype(jnp.float32))
    dt = jnp.exp(p["s5_log_dt"][d].astype(jnp.float32))[:, None]
    a_bar = jnp.exp(lam * dt)
    b_mat = lax.complex(p["s5_b_re"][d].astype(jnp.float32), p["s5_b_im"][d].astype(jnp.float32))
    b_bar = ((a_bar - 1.0) / lam)[..., None] * b_mat
    bu = jnp.einsum("blgp,gnp->blgn", u.astype(jnp.complex64), b_bar)
    first, last = (-1, 0) if reverse else (0, -1)
    bu = bu.at[:, first].add(a_bar[None] * h0)
    a_seq = jnp.broadcast_to(a_bar, bu.shape)
    _, states = lax.associative_scan(_ssm_combine, (a_seq, bu), axis=1, reverse=reverse)
    c_mat = lax.complex(p["s5_c_re"][d].astype(jnp.float32), p["s5_c_im"][d].astype(jnp.float32))
    y = jnp.einsum("blgn,gpn->blgp", states, c_mat).real
    return y, states[:, last]


def _s5(u, p, h0):
    b, L = u.shape[0], u.shape[1]
    uf = u.astype(jnp.float32)
    ug = uf.reshape(b, L, S5_GROUPS, S5_GROUP)
    y_f, h_f = _s5_direction(ug, p, 0, h0[:, 0], False)
    y_b, h_b = _s5_direction(ug, p, 1, h0[:, 1], True)
    y = (y_f + y_b).reshape(b, L, S5_WIDTH) + uf * p["s5_d"].astype(jnp.float32)
    y = jax.nn.gelu(y)
    g = y @ p["s5_glu_w"].astype(jnp.float32) + p["s5_glu_b"].astype(jnp.float32)
    ga, gb = jnp.split(g, 2, axis=-1)
    return (ga * jax.nn.sigmoid(gb)).astype(u.dtype), jnp.stack([h_f, h_b], axis=1)


def _moe(x, router_w, router_b, w1, b1, w2, b2):
    T, D = x.shape
    logits = (x @ router_w + router_b).astype(jnp.float32)
    top_v, top_i = lax.top_k(logits, TOP_K)
    gates = jax.nn.softmax(top_v, axis=-1)
    A = T * TOP_K
    e_flat = top_i.reshape(-1)
    tok_flat = jnp.repeat(jnp.arange(T, dtype=jnp.int32), TOP_K)
    g_flat = gates.reshape(-1)
    order = jnp.argsort(e_flat)
    e_sorted = e_flat[order]
    counts = jnp.bincount(e_flat, length=N_EXPERTS)
    padded = ((counts + MOE_BLOCK - 1) // MOE_BLOCK) * MOE_BLOCK
    start = jnp.cumsum(counts) - counts
    pend = jnp.cumsum(padded)
    pstart = pend - padded
    dest = pstart[e_sorted] + (jnp.arange(A) - start[e_sorted])
    P = A + N_EXPERTS * MOE_BLOCK
    nblk = P // MOE_BLOCK
    slot_tok = jnp.full((P,), T, jnp.int32).at[dest].set(tok_flat[order])
    slot_gate = jnp.zeros((P,), jnp.float32).at[dest].set(g_flat[order])
    blk_expert = jnp.minimum(jnp.searchsorted(pend, jnp.arange(nblk) * MOE_BLOCK, side="right"),
                             N_EXPERTS - 1)
    x_pad = jnp.concatenate([x, jnp.zeros((1, D), x.dtype)], axis=0)
    xs = x_pad[slot_tok].reshape(nblk, MOE_BLOCK, D)

    def expert_block(args):
        xb, e = args
        h = xb @ w1[e] + b1[e]
        glu = jnp.minimum(h[:, :D_EXPERT], SWIGLU_LIMIT)
        lin = jnp.clip(h[:, D_EXPERT:], -SWIGLU_LIMIT, SWIGLU_LIMIT)
        o = glu * jax.nn.sigmoid(SWIGLU_ALPHA * glu) * (lin + 1.0)
        return o @ w2[e] + b2[e]

    ys = lax.map(expert_block, (xs, blk_expert)).reshape(P, D)
    out = jnp.zeros((T + 1, D), jnp.float32).at[slot_tok].add(ys.astype(jnp.float32) * slot_gate[:, None])
    return out[:T].astype(x.dtype)


def _mixer(h, p, ctx):
    b, L, _ = h.shape
    proj = h @ p["w_in"]
    o1 = 3 * HY_WIDTH
    o2 = o1 + ATT_WIDTH
    o3 = o2 + KV_WIDTH
    o4 = o3 + KV_WIDTH
    o5 = o4 + S5_WIDTH
    hy_in, q, k, v, s_in, gates = jnp.split(proj, [o1, o2, o3, o4, o5], axis=-1)
    q = q.reshape(b, L, N_HEADS, HEAD_DIM)
    k = k.reshape(b, L, N_KV_HEADS, HEAD_DIM)
    v = v.reshape(b, L, N_KV_HEADS, HEAD_DIM)
    if ctx is None:
        att = _context_attention(q, k, v, p["attn_sink"])
        h0 = jnp.zeros((b, 2, S5_GROUPS, S5_STATE), jnp.complex64)
    else:
        ck, cv, h0 = ctx
        att = _latent_attention(_axial_rope(q), _axial_rope(k), v, ck, cv, p["attn_sink"])
    hy = _hyena(hy_in, p)
    s, s_final = _s5(s_in, p, h0)
    g = jax.nn.sigmoid(gates.astype(jnp.float32)).reshape(b, L, N_BRANCH, D_MODEL)
    merged = (g[:, :, 0] * (hy @ p["w_br_h"]) + g[:, :, 1] * (att @ p["w_br_a"])
              + g[:, :, 2] * (s @ p["w_br_s"]))
    return merged.astype(h.dtype) @ p["w_out"], (k, v, s_final)


def _layer(x, mod, p, ctx):
    sh1, sc1, g1, sh2, sc2, g2 = mod
    h = _rmsnorm(x, p["norm_pre"][0]) * (1.0 + sc1) + sh1
    m, side = _mixer(h, p, ctx)
    x = x + g1 * _rmsnorm(m, p["norm_post"][0])
    h = _rmsnorm(x, p["norm_pre"][1]) * (1.0 + sc2) + sh2
    b, L, D = h.shape
    f = _moe(h.reshape(b * L, D), p["router_w"], p["router_b"], p["exp_w1"], p["exp_b1"],
             p["exp_w2"], p["exp_b2"]).reshape(b, L, D)
    x = x + g2 * _rmsnorm(f, p["norm_post"][1])
    return x, side


def setup_inputs(seed: int = 0) -> dict:
    key = jax.random.key(seed)
    ks = iter(jax.random.split(key, 48))
    f32 = jnp.float32

    def nrm(shape, scale):
        return scale * jax.random.normal(next(ks), shape, f32)

    def uni(shape, lo, hi):
        return jax.random.uniform(next(ks), shape, f32, lo, hi)

    n_idx = jnp.arange(S5_STATE, dtype=f32)
    return {
        "x_prompt": nrm((BATCH, SEQ, D_MODEL), 1.0),
        "x_sample": nrm((DEC_BATCH, DEC_SEQ, D_MODEL), 1.0),
        "cache_k": nrm((DEC_BATCH, DEPTH, PAST_LEN, N_KV_HEADS, HEAD_DIM), 1.0),
        "cache_v": nrm((DEC_BATCH, DEPTH, PAST_LEN, N_KV_HEADS, HEAD_DIM), 1.0),
        "state_ssm": nrm((DEC_BATCH, DEPTH, 2, S5_GROUPS, S5_STATE, 2), 0.1),
        "c": nrm((DEC_BATCH, D_MODEL), 1.0),
        "c_ctx": nrm((D_MODEL,), 1.0),
        "w_mod": nrm((DEPTH, D_MODEL, 6 * D_MODEL), 0.5 * D_MODEL ** -0.5),
        "b_mod": nrm((DEPTH, 6 * D_MODEL), 0.02),
        "norm_pre": 1.0 + nrm((DEPTH, 2, D_MODEL), 0.05),
        "norm_post": 1.0 + nrm((DEPTH, 2, D_MODEL), 0.05),
        "w_in": nrm((DEPTH, D_MODEL, IN_WIDTH), D_MODEL ** -0.5),
        "hy_short_w": nrm((DEPTH, 3, 3 * HY_WIDTH), 3 ** -0.5),
        "hy_short_b": nrm((DEPTH, 3 * HY_WIDTH), 0.02),
        "hy_f_w1": nrm((DEPTH, HY_EMB, HY_HIDDEN), HY_EMB ** -0.5),
        "hy_f_b1": nrm((DEPTH, HY_HIDDEN), 0.1),
        "hy_f_freq": 1.0 + nrm((DEPTH, 2, HY_HIDDEN), 0.1),
        "hy_f_w2": nrm((DEPTH, HY_HIDDEN, HY_HIDDEN), HY_HIDDEN ** -0.5),
        "hy_f_b2": nrm((DEPTH, HY_HIDDEN), 0.1),
        "hy_f_w3": nrm((DEPTH, HY_HIDDEN, HY_ORDER * 2 * HY_WIDTH), HY_HIDDEN ** -0.5),
        "hy_decay": uni((DEPTH, HY_ORDER, 2, HY_WIDTH), math.log(100.0) / 1.5, math.log(100.0) / 0.3),
        "hy_bias": nrm((DEPTH, HY_ORDER, HY_WIDTH), 0.5),
        "attn_sink": nrm((DEPTH, N_HEADS), 1.0),
        "s5_a_re": -0.5 + nrm((DEPTH, 2, S5_GROUPS, S5_STATE), 0.01),
        "s5_a_im": math.pi * n_idx + nrm((DEPTH, 2, S5_GROUPS, S5_STATE), 0.01),
        "s5_log_dt": uni((DEPTH, 2, S5_GROUPS), math.log(0.001), math.log(0.1)),
        "s5_b_re": nrm((DEPTH, 2, S5_GROUPS, S5_STATE, S5_GROUP), (2 * S5_GROUP) ** -0.5),
        "s5_b_im": nrm((DEPTH, 2, S5_GROUPS, S5_STATE, S5_GROUP), (2 * S5_GROUP) ** -0.5),
        "s5_c_re": nrm((DEPTH, 2, S5_GROUPS, S5_GROUP, S5_STATE), S5_STATE ** -0.5),
        "s5_c_im": nrm((DEPTH, 2, S5_GROUPS, S5_GROUP, S5_STATE), S5_STATE ** -0.5),
        "s5_d": nrm((DEPTH, S5_WIDTH), 1.0),
        "s5_glu_w": nrm((DEPTH, S5_WIDTH, 2 * S5_WIDTH), S5_WIDTH ** -0.5),
        "s5_glu_b": nrm((DEPTH, 2 * S5_WIDTH), 0.02),
        "w_br_h": nrm((DEPTH, HY_WIDTH, D_MODEL), HY_WIDTH ** -0.5),
        "w_br_a": nrm((DEPTH, ATT_WIDTH, D_MODEL), ATT_WIDTH ** -0.5),
        "w_br_s": nrm((DEPTH, S5_WIDTH, D_MODEL), S5_WIDTH ** -0.5),
        "w_out": nrm((DEPTH, D_MODEL, D_MODEL), D_MODEL ** -0.5),
        "router_w": nrm((DEPTH, D_MODEL, N_EXPERTS), D_MODEL ** -0.5),
        "router_b": nrm((DEPTH, N_EXPERTS), 0.01),
        "exp_w1": nrm((DEPTH, N_EXPERTS, D_MODEL, 2 * D_EXPERT), D_MODEL ** -0.5),
        "exp_b1": nrm((DEPTH, N_EXPERTS, 2 * D_EXPERT), 0.02),
        "exp_w2": nrm((DEPTH, N_EXPERTS, D_EXPERT, D_MODEL), D_EXPERT ** -0.5),
        "exp_b2": nrm((DEPTH, N_EXPERTS, D_MODEL), 0.02),
    }


def reference(x_prompt, x_sample, cache_k, cache_v, state_ssm, c, c_ctx, w_mod, b_mod, norm_pre,
              norm_post, w_in, hy_short_w, hy_short_b, hy_f_w1, hy_f_b1, hy_f_freq, hy_f_w2, hy_f_b2,
              hy_f_w3, hy_decay, hy_bias, attn_sink, s5_a_re, s5_a_im, s5_log_dt, s5_b_re, s5_b_im,
              s5_c_re, s5_c_im, s5_d, s5_glu_w, s5_glu_b, w_br_h, w_br_a, w_br_s, w_out, router_w,
              router_b, exp_w1, exp_b1, exp_w2, exp_b2):
    x_ctx = x_prompt
    x_lat = x_sample
    new_k, new_v, new_s = [], [], []
    for l in range(DEPTH):
        p = dict(w_mod=w_mod[l], b_mod=b_mod[l], norm_pre=norm_pre[l], norm_post=norm_post[l],
                 w_in=w_in[l], hy_short_w=hy_short_w[l], hy_short_b=hy_short_b[l],
                 hy_f_w1=hy_f_w1[l], hy_f_b1=hy_f_b1[l], hy_f_freq=hy_f_freq[l], hy_f_w2=hy_f_w2[l],
                 hy_f_b2=hy_f_b2[l], hy_f_w3=hy_f_w3[l], hy_decay=hy_decay[l], hy_bias=hy_bias[l],
                 attn_sink=attn_sink[l], s5_a_re=s5_a_re[l], s5_a_im=s5_a_im[l],
                 s5_log_dt=s5_log_dt[l], s5_b_re=s5_b_re[l], s5_b_im=s5_b_im[l],
                 s5_c_re=s5_c_re[l], s5_c_im=s5_c_im[l], s5_d=s5_d[l], s5_glu_w=s5_glu_w[l],
                 s5_glu_b=s5_glu_b[l], w_br_h=w_br_h[l], w_br_a=w_br_a[l], w_br_s=w_br_s[l],
                 w_out=w_out[l], router_w=router_w[l], router_b=router_b[l], exp_w1=exp_w1[l],
                 exp_b1=exp_b1[l], exp_w2=exp_w2[l], exp_b2=exp_b2[l])
        mod_ctx = _modulation(c_ctx[None, :], p["w_mod"], p["b_mod"])
        x_ctx, (k_l, v_l, s_l) = _layer(x_ctx, mod_ctx, p, None)
        new_k.append(k_l)
        new_v.append(v_l)
        new_s.append(jnp.stack([s_l.real, s_l.imag], axis=-1).astype(x_prompt.dtype))
        st = state_ssm[:, l].astype(jnp.float32)
        h0 = lax.complex(st[..., 0], st[..., 1])
        mod_lat = _modulation(c, p["w_mod"], p["b_mod"])
        x_lat, _ = _layer(x_lat, mod_lat, p, (cache_k[:, l], cache_v[:, l], h0))
    new_cache_k = jnp.stack(new_k, axis=1)
    new_cache_v = jnp.stack(new_v, axis=1)
    new_state_ssm = jnp.stack(new_s, axis=1)
    return (x_ctx, x_lat, new_cache_k, new_cache_v, new_state_ssm)
```

```python
import functools
import math

import jax
import jax.numpy as jnp
import numpy as np
from jax import lax
from jax.experimental import pallas as pl
from jax.experimental.pallas import tpu as pltpu

F32 = jnp.float32
BF16 = jnp.bfloat16

D_MODEL = 1024
BATCH = 16
SEQ = 256
DEPTH = 2
DEC_BATCH = 4
DEC_SEQ = 4096
PAST_LEN = 256
GRID_W = 64
HY_WIDTH = D_MODEL // 4
HEAD_DIM = 64
N_HEADS = (D_MODEL // 2) // HEAD_DIM
N_KV_HEADS = N_HEADS // 4
GQ = N_HEADS // N_KV_HEADS
ATT_WIDTH = N_HEADS * HEAD_DIM
KV_WIDTH = N_KV_HEADS * HEAD_DIM
S5_WIDTH = D_MODEL // 4
S5_GROUP = 16
S5_GROUPS = S5_WIDTH // S5_GROUP
S5_STATE = 64
N_BRANCH = 3
HY_ORDER = 2
HY_BANDS = 16
WINDOW = 128
Q_BLOCK = 128
ROPE_BASE = 10000.0
NEG_INF = -1e30
N_EXPERTS = 32
TOP_K = 4
D_EXPERT = D_MODEL
SWIGLU_LIMIT = 7.0
SWIGLU_ALPHA = 1.702
RMS_EPS = 1e-6

T_CTX = BATCH * SEQ
T_LAT = DEC_BATCH * DEC_SEQ
T_ALL = T_CTX + T_LAT
SEG = 4096
N_SEG = T_ALL // SEG
assert T_CTX == SEG and DEC_SEQ == SEG

O_HY = 3 * HY_WIDTH
O_Q = O_HY + ATT_WIDTH
O_K = O_Q + KV_WIDTH
O_V = O_K + KV_WIDTH
O_S = O_V + S5_WIDTH
MIX_WIDTH = O_S

VMEM_LIMIT = 56 * 1024 * 1024

TM_PROJ = 512
TM_MERGE = 256
TM_MOE = 512


def _cparams(sem, vmem=VMEM_LIMIT):
    return pltpu.CompilerParams(dimension_semantics=sem, vmem_limit_bytes=vmem)


def _rms(x, g):
    return x * lax.rsqrt(jnp.mean(x * x, axis=-1, keepdims=True) + RMS_EPS) * g


def _mod_body(c_ref, w_ref, b_ref, o_ref):
    c = c_ref[...]
    a = c * jax.nn.sigmoid(c)
    o_ref[0] = jnp.dot(a, w_ref[0], preferred_element_type=F32,
                       precision=lax.Precision.HIGHEST) + b_ref[0]


def _modulation(cond8, w_mod, b_mod):
    tn = 1536
    n = 6 * D_MODEL
    return pl.pallas_call(
        _mod_body,
        grid=(DEPTH, n // tn),
        in_specs=[pl.BlockSpec((8, D_MODEL), lambda l, j: (0, 0)),
                  pl.BlockSpec((1, D_MODEL, tn), lambda l, j: (l, 0, j)),
                  pl.BlockSpec((1, 1, tn), lambda l, j: (l, 0, j))],
        out_specs=pl.BlockSpec((1, 8, tn), lambda l, j: (l, 0, j)),
        out_shape=jax.ShapeDtypeStruct((DEPTH, 8, n), F32),
        compiler_params=_cparams(("arbitrary", "arbitrary")),
        name="modulation",
    )(cond8, w_mod, b_mod.reshape(DEPTH, 1, n))


def _rope_tables():
    t = np.arange(SEG)
    row = (t // GRID_W).astype(np.float64)
    col = (t % GRID_W).astype(np.float64)
    quarter = HEAD_DIM // 4
    inv = ROPE_BASE ** (-np.arange(quarter, dtype=np.float64) / quarter)
    lane = np.arange(128)
    d = lane % HEAD_DIM
    pos = np.where((d // 32)[None, :] == 0, row[:, None], col[:, None])
    ang = pos * inv[d % quarter][None, :]
    first = ((d % 32) < quarter)[None, :]
    cos = np.cos(ang)
    sin = np.sin(ang)
    sin_a = np.where(first, -sin, 0.0)
    sin_b = np.where(first, 0.0, sin)
    ident = np.zeros((SEG, 128))
    tab = lambda ctx, lat: jnp.asarray(np.concatenate([ctx, lat], axis=0), F32)
    return tab(ident + 1.0, cos), tab(ident, sin_a), tab(ident, sin_b)


def _rope(x, cos, sin_a, sin_b):
    w = x.shape[-1]
    rep = w // 128
    if rep > 1:
        cos = jnp.concatenate([cos] * rep, axis=1)
        sin_a = jnp.concatenate([sin_a] * rep, axis=1)
        sin_b = jnp.concatenate([sin_b] * rep, axis=1)
    quarter = HEAD_DIM // 4
    return x * cos + pltpu.roll(x, w - quarter, 1) * sin_a + pltpu.roll(x, quarter, 1) * sin_b


def _proj_body(x_ref, g_ref, sh_ref, sc_ref, w_ref, cos_ref, sa_ref, sb_ref,
               hy_ref, q_ref, k_ref, v_ref, s_ref):
    h = _rms(x_ref[...], g_ref[...]) * (1.0 + sc_ref[0]) + sh_ref[0]
    p = jnp.dot(h.astype(BF16), w_ref[...], preferred_element_type=F32)
    cos, sa, sb = cos_ref[...], sa_ref[...], sb_ref[...]
    hy_ref[...] = p[:, :O_HY]
    q_ref[...] = _rope(p[:, O_HY:O_Q], cos, sa, sb)
    k_ref[...] = _rope(p[:, O_Q:O_K], cos, sa, sb)
    v_ref[...] = p[:, O_K:O_V]
    s_ref[...] = p[:, O_V:O_S]


def _seg_spec(tm):
    return pl.BlockSpec((1, 1, D_MODEL), lambda i: ((i * tm) // SEG, 0, 0))


def _const_spec(shape):
    nd = len(shape)
    return pl.BlockSpec(shape, lambda i: (0,) * nd, pipeline_mode=pl.Buffered(1))


def _in_proj(x, gain, sh, sc, w_mix, rope_tabs):
    tm = TM_PROJ
    per_seg = SEG // tm
    rope_spec = pl.BlockSpec((tm, 128), lambda i: (jnp.where(i < per_seg, i, per_seg + i % per_seg), 0))
    tok = lambda w: pl.BlockSpec((tm, w), lambda i: (i, 0))
    widths = (O_HY, ATT_WIDTH, KV_WIDTH, KV_WIDTH, S5_WIDTH)
    return pl.pallas_call(
        _proj_body,
        grid=(T_ALL // tm,),
        in_specs=[tok(D_MODEL), _const_spec((1, D_MODEL)), _seg_spec(tm), _seg_spec(tm),
                  _const_spec((D_MODEL, MIX_WIDTH)), rope_spec, rope_spec, rope_spec],
        out_specs=[tok(w) for w in widths],
        out_shape=[jax.ShapeDtypeStruct((T_ALL, w), F32) for w in widths],
        compiler_params=_cparams(("arbitrary",)),
        name="in_proj",
    )(x, gain, sh, sc, w_mix, *rope_tabs)


def _attend(q, keys, vals, masks, sink_ref):
    qb = q.shape[0]
    scale = HEAD_DIM ** -0.5
    grp = lax.broadcasted_iota(jnp.int32, (GQ * qb, 1), 0) // qb
    outs = []
    for h in range(N_KV_HEADS):
        lo = h * HEAD_DIM
        qs = jnp.concatenate([q[:, (h * GQ + g) * HEAD_DIM:(h * GQ + g + 1) * HEAD_DIM]
                              for g in range(GQ)], axis=0)
        qs = (qs * scale).astype(BF16)
        sink = jnp.zeros((GQ * qb, 1), F32)
        for g in range(GQ):
            sink = jnp.where(grp == g, sink_ref[h * GQ + g], sink)
        logits = []
        for k_i, m_i in zip(keys, masks):
            s = lax.dot_general(qs, k_i[:, lo:lo + HEAD_DIM].astype(BF16),
                                (((1,), (1,)), ((), ())), preferred_element_type=F32)
            if m_i is not None:
                s = jnp.where(jnp.concatenate([m_i] * GQ, axis=0), s, NEG_INF)
            logits.append(s)
        m = sink
        for s in logits:
            m = jnp.maximum(m, jnp.max(s, axis=-1, keepdims=True))
        denom = jnp.exp(sink - m)
        acc = jnp.zeros((GQ * qb, HEAD_DIM), F32)
        for s, v_i in zip(logits, vals):
            p = jnp.exp(s - m)
            denom = denom + jnp.sum(p, axis=-1, keepdims=True)
            acc = acc + jnp.dot(p.astype(BF16), v_i[:, lo:lo + HEAD_DIM].astype(BF16),
                                preferred_element_type=F32)
        o = acc / denom
        outs.extend(o[g * qb:(g + 1) * qb] for g in range(GQ))
    return jnp.concatenate(outs, axis=1)


def _ctx_attn_body(sink_ref, q_ref, k_ref, v_ref, o_ref):
    o_ref[0] = _attend(q_ref[0], [k_ref[0]], [v_ref[0]], [None], sink_ref)


def _lat_attn_body(sink_ref, q_ref, kp_ref, kc_ref, kn_ref, vp_ref, vc_ref, vn_ref, ck_ref, cv_ref, o_ref):
    i = pl.program_id(1)
    kw = jnp.concatenate([kp_ref[0], kc_ref[0], kn_ref[0]], axis=0)
    vw = jnp.concatenate([vp_ref[0], vc_ref[0], vn_ref[0]], axis=0)
    qpos = i * Q_BLOCK + lax.broadcasted_iota(jnp.int32, (Q_BLOCK, 3 * Q_BLOCK), 0)
    kpos = (i - 1) * Q_BLOCK + lax.broadcasted_iota(jnp.int32, (Q_BLOCK, 3 * Q_BLOCK), 1)
    band = (jnp.abs(qpos - kpos) <= WINDOW) & (kpos >= 0) & (kpos < DEC_SEQ)
    o_ref[0] = _attend(q_ref[0], [kw, ck_ref[0]], [vw, cv_ref[0]], [band, None], sink_ref)


def _attention(q, k, v, ck, cv, sink):
    smem = pl.BlockSpec(memory_space=pltpu.SMEM)
    qc = q[:T_CTX].reshape(BATCH, SEQ, ATT_WIDTH)
    kc = k[:T_CTX].reshape(BATCH, SEQ, KV_WIDTH)
    vc = v[:T_CTX].reshape(BATCH, SEQ, KV_WIDTH)
    kv_spec = pl.BlockSpec((1, SEQ, KV_WIDTH), lambda b, i: (b, 0, 0))
    att_c = pl.pallas_call(
        _ctx_attn_body,
        grid=(BATCH, SEQ // Q_BLOCK),
        in_specs=[smem, pl.BlockSpec((1, Q_BLOCK, ATT_WIDTH), lambda b, i: (b, i, 0)), kv_spec, kv_spec],
        out_specs=pl.BlockSpec((1, Q_BLOCK, ATT_WIDTH), lambda b, i: (b, i, 0)),
        out_shape=jax.ShapeDtypeStruct((BATCH, SEQ, ATT_WIDTH), F32),
        compiler_params=_cparams(("arbitrary", "arbitrary")),
        name="ctx_attention",
    )(sink, qc, kc, vc)
    nb = DEC_SEQ // Q_BLOCK
    ql = q[T_CTX:].reshape(DEC_BATCH, DEC_SEQ, ATT_WIDTH)
    kl = k[T_CTX:].reshape(DEC_BATCH, DEC_SEQ, KV_WIDTH)
    vl = v[T_CTX:].reshape(DEC_BATCH, DEC_SEQ, KV_WIDTH)
    blk = lambda f: pl.BlockSpec((1, Q_BLOCK, KV_WIDTH), lambda b, i: (b, f(i), 0))
    prev, cur, nxt = blk(lambda i: jnp.maximum(i - 1, 0)), blk(lambda i: i), blk(lambda i: jnp.minimum(i + 1, nb - 1))
    cache_spec = pl.BlockSpec((1, PAST_LEN, KV_WIDTH), lambda b, i: (b, 0, 0))
    att_l = pl.pallas_call(
        _lat_attn_body,
        grid=(DEC_BATCH, nb),
        in_specs=[smem, pl.BlockSpec((1, Q_BLOCK, ATT_WIDTH), lambda b, i: (b, i, 0)),
                  prev, cur, nxt, prev, cur, nxt, cache_spec, cache_spec],
        out_specs=pl.BlockSpec((1, Q_BLOCK, ATT_WIDTH), lambda b, i: (b, i, 0)),
        out_shape=jax.ShapeDtypeStruct((DEC_BATCH, DEC_SEQ, ATT_WIDTH), F32),
        compiler_params=_cparams(("arbitrary", "arbitrary")),
        name="lat_attention",
    )(sink, ql, kl, kl, kl, vl, vl, vl, ck, cv)
    return jnp.concatenate([att_c.reshape(T_CTX, ATT_WIDTH), att_l.reshape(T_LAT, ATT_WIDTH)], axis=0)


def _short_conv(x, w, b):
    xp = jnp.pad(x, ((0, 0), (1, 1), (0, 0)))
    return xp[:, :-2] * w[0] + xp[:, 1:-1] * w[1] + xp[:, 2:] * w[2] + b


def _hyena_filters(L, p):
    t = jnp.arange(L, dtype=F32)
    t_norm = t / max(L - 1, 1)
    bands = jnp.arange(1, HY_BANDS + 1, dtype=F32)
    ang = (2.0 * math.pi / L) * t[:, None] * bands[None, :]
    z = jnp.concatenate([t_norm[:, None], jnp.cos(ang), jnp.sin(ang)], axis=-1)
    h = jnp.sin(p["hy_f_freq"][0] * (z @ p["hy_f_w1"] + p["hy_f_b1"]))
    h = jnp.sin(p["hy_f_freq"][1] * (h @ p["hy_f_w2"] + p["hy_f_b2"]))
    h = (h @ p["hy_f_w3"]).astype(F32).reshape(L, HY_ORDER, 2, HY_WIDTH)
    h = h * jnp.exp(-t_norm[:, None, None, None] * jnp.abs(p["hy_decay"].astype(F32))[None])
    return h / (jnp.sum(jnp.abs(h), axis=(0, 2), keepdims=True) + 1e-6)


def _bidir_fftconv(u, h_fwd, h_bwd, bias):
    L, C = u.shape[1], u.shape[2]
    k = jnp.concatenate([h_fwd, jnp.zeros((1, C), F32), h_bwd[:0:-1]], axis=0)
    uf = jnp.fft.rfft(u.astype(F32), n=2 * L, axis=1)
    kf = jnp.fft.rfft(k, n=2 * L, axis=0)
    y = jnp.fft.irfft(uf * kf[None], n=2 * L, axis=1)[:, :L]
    return y + u * bias


def _hyena_group(proj_h, p):
    L = proj_h.shape[1]
    uc = _short_conv(proj_h, p["hy_short_w"], p["hy_short_b"])
    x1, x2, v = jnp.split(uc, 3, axis=-1)
    h = _hyena_filters(L, p)
    z = x1 * _bidir_fftconv(v, h[:, 0, 0], h[:, 0, 1], p["hy_bias"][0])
    z = x2 * _bidir_fftconv(z, h[:, 1, 0], h[:, 1, 1], p["hy_bias"][1])
    return z


def _hyena(hy_in, p):
    zc = _hyena_group(hy_in[:T_CTX].reshape(BATCH, SEQ, O_HY), p)
    zl = _hyena_group(hy_in[T_CTX:].reshape(DEC_BATCH, DEC_SEQ, O_HY), p)
    return jnp.concatenate([zc.reshape(T_CTX, HY_WIDTH), zl.reshape(T_LAT, HY_WIDTH)], axis=0)


def _ssm_combine(e1, e2):
    a1, b1 = e1
    a2, b2 = e2
    return a1 * a2, a2 * b1 + b2


def _s5_direction(u, p, d, h0, reverse):
    lam = lax.complex(jnp.minimum(p["s5_a_re"][d], -1e-4), p["s5_a_im"][d])
    dt = jnp.exp(p["s5_log_dt"][d])[:, None]
    a_bar = jnp.exp(lam * dt)
    b_mat = lax.complex(p["s5_b_re"][d], p["s5_b_im"][d])
    b_bar = ((a_bar - 1.0) / lam)[..., None] * b_mat
    bu = jnp.einsum("blgp,gnp->blgn", u.astype(jnp.complex64), b_bar)
    first, last = (-1, 0) if reverse else (0, -1)
    bu = bu.at[:, first].add(a_bar[None] * h0)
    a_seq = jnp.broadcast_to(a_bar, bu.shape)
    _, states = lax.associative_scan(_ssm_combine, (a_seq, bu), axis=1, reverse=reverse)
    c_mat = lax.complex(p["s5_c_re"][d], p["s5_c_im"][d])
    y = jnp.einsum("blgn,gpn->blgp", states, c_mat).real
    return y, states[:, last]


def _s5_group(u, p, h0):
    b, L = u.shape[0], u.shape[1]
    ug = u.reshape(b, L, S5_GROUPS, S5_GROUP)
    y_f, h_f = _s5_direction(ug, p, 0, h0[:, 0], False)
    y_b, h_b = _s5_direction(ug, p, 1, h0[:, 1], True)
    return y_f.reshape(b, L, S5_WIDTH), y_b.reshape(b, L, S5_WIDTH), jnp.stack([h_f, h_b], axis=1)


def _s5_scan(s_in, p, h0_lat):
    h0_ctx = jnp.zeros((BATCH, 2, S5_GROUPS, S5_STATE), jnp.complex64)
    yf_c, yb_c, hfin = _s5_group(s_in[:T_CTX].reshape(BATCH, SEQ, S5_WIDTH), p, h0_ctx)
    yf_l, yb_l, _ = _s5_group(s_in[T_CTX:].reshape(DEC_BATCH, DEC_SEQ, S5_WIDTH), p, h0_lat)
    yf = jnp.concatenate([yf_c.reshape(T_CTX, S5_WIDTH), yf_l.reshape(T_LAT, S5_WIDTH)], axis=0)
    yb = jnp.concatenate([yb_c.reshape(T_CTX, S5_WIDTH), yb_l.reshape(T_LAT, S5_WIDTH)], axis=0)
    return yf, yb, hfin


def _merge_body(x_ref, hy_ref, att_ref, yf_ref, yb_ref, sin_ref,
                sh1_ref, sc1_ref, g1_ref, sh2_ref, sc2_ref,
                np0_ref, npost0_ref, np1_ref, d_ref, gluw_ref, glub_ref,
                wg_ref, wh_ref, wa_ref, ws_ref, wo_ref, rw_ref, rb_ref,
                x1_ref, h2_ref, ti_ref, tg_ref):
    x = x_ref[...]
    h = (_rms(x, np0_ref[...]) * (1.0 + sc1_ref[0]) + sh1_ref[0]).astype(BF16)
    dot = functools.partial(jnp.dot, preferred_element_type=F32)
    ys = yf_ref[...] + yb_ref[...] + sin_ref[...] * d_ref[...]
    ys = jax.nn.gelu(ys)
    g = dot(ys.astype(BF16), gluw_ref[...]) + glub_ref[...]
    s = g[:, :S5_WIDTH] * jax.nn.sigmoid(g[:, S5_WIDTH:])
    merged = jax.nn.sigmoid(dot(h, wg_ref[:, :D_MODEL])) * dot(hy_ref[...].astype(BF16), wh_ref[...])
    merged += jax.nn.sigmoid(dot(h, wg_ref[:, D_MODEL:2 * D_MODEL])) * dot(att_ref[...].astype(BF16), wa_ref[...])
    merged += jax.nn.sigmoid(dot(h, wg_ref[:, 2 * D_MODEL:])) * dot(s.astype(BF16), ws_ref[...])
    m = dot(merged.astype(BF16), wo_ref[...])
    x1 = x + g1_ref[0] * _rms(m, npost0_ref[...])
    x1_ref[...] = x1
    h2 = _rms(x1, np1_ref[...]) * (1.0 + sc2_ref[0]) + sh2_ref[0]
    h2_ref[...] = h2.astype(BF16)
    logits = jnp.dot(h2, rw_ref[...], preferred_element_type=F32,
                     precision=lax.Precision.HIGHEST) + rb_ref[...]
    lane = lax.broadcasted_iota(jnp.int32, logits.shape, 1)
    vals, idxs = [], []
    for _ in range(TOP_K):
        mx = jnp.max(logits, axis=-1, keepdims=True)
        ix = jnp.min(jnp.where(logits == mx, lane, N_EXPERTS), axis=-1, keepdims=True)
        vals.append(mx)
        idxs.append(ix)
        logits = jnp.where(lane == ix, -jnp.inf, logits)
    e = [jnp.exp(v - vals[0]) for v in vals]
    tot = e[0] + e[1] + e[2] + e[3]
    ti_ref[...] = jnp.concatenate(idxs, axis=1)
    tg_ref[...] = jnp.concatenate(e, axis=1) / tot


def _merge(x, hy, att, yf, yb, s_in, mods, vecs, mats):
    tm = TM_MERGE
    tok = lambda w: pl.BlockSpec((tm, w), lambda i: (i, 0))
    in_specs = ([tok(D_MODEL), tok(HY_WIDTH), tok(ATT_WIDTH), tok(S5_WIDTH), tok(S5_WIDTH), tok(S5_WIDTH)]
                + [_seg_spec(tm)] * len(mods)
                + [_const_spec(v.shape) for v in vecs]
                + [_const_spec(m.shape) for m in mats])
    return pl.pallas_call(
        _merge_body,
        grid=(T_ALL // tm,),
        in_specs=in_specs,
        out_specs=[tok(D_MODEL), tok(D_MODEL), tok(TOP_K), tok(TOP_K)],
        out_shape=[jax.ShapeDtypeStruct((T_ALL, D_MODEL), F32),
                   jax.ShapeDtypeStruct((T_ALL, D_MODEL), BF16),
                   jax.ShapeDtypeStruct((T_ALL, TOP_K), jnp.int32),
                   jax.ShapeDtypeStruct((T_ALL, TOP_K), F32)],
        compiler_params=_cparams(("arbitrary",)),
        name="merge_router",
    )(x, hy, att, yf, yb, s_in, *mods, *vecs, *mats)


def _moe_body(be_ref, first_ref, nused_ref, xs_ref, gate_ref, w1_ref, b1_ref, w2_ref, b2_ref,
              o_ref, w1b_ref, w2b_ref):
    i = pl.program_id(0)

    @pl.when(first_ref[i] == 1)
    def _():
        w1b_ref[...] = w1_ref[0].astype(BF16)
        w2b_ref[...] = w2_ref[0].astype(BF16)

    @pl.when(i < nused_ref[0])
    def _():
        h = jnp.dot(xs_ref[...], w1b_ref[...], preferred_element_type=F32) + b1_ref[0]
        glu = jnp.minimum(h[:, :D_EXPERT], SWIGLU_LIMIT)
        lin = jnp.clip(h[:, D_EXPERT:], -SWIGLU_LIMIT, SWIGLU_LIMIT)
        o = glu * jax.nn.sigmoid(SWIGLU_ALPHA * glu) * (lin + 1.0)
        y = jnp.dot(o.astype(BF16), w2b_ref[...], preferred_element_type=F32) + b2_ref[0]
        o_ref[...] = y * gate_ref[...]

    @pl.when(i >= nused_ref[0])
    def _():
        o_ref[...] = jnp.zeros_like(o_ref)


def _moe_experts(xs, slot_gate, blk_expert, blk_first, n_used, w1, b1, w2, b2):
    tm = TM_MOE
    n_slots = xs.shape[0]
    grid_spec = pltpu.PrefetchScalarGridSpec(
        num_scalar_prefetch=3,
        grid=(n_slots // tm,),
        in_specs=[pl.BlockSpec((tm, D_MODEL), lambda i, be, fi, nu: (i, 0)),
                  pl.BlockSpec((tm, 1), lambda i, be, fi, nu: (i, 0)),
                  pl.BlockSpec((1, D_MODEL, 2 * D_EXPERT), lambda i, be, fi, nu: (be[i], 0, 0)),
                  pl.BlockSpec((1, 1, 2 * D_EXPERT), lambda i, be, fi, nu: (be[i], 0, 0)),
                  pl.BlockSpec((1, D_EXPERT, D_MODEL), lambda i, be, fi, nu: (be[i], 0, 0)),
                  pl.BlockSpec((1, 1, D_MODEL), lambda i, be, fi, nu: (be[i], 0, 0))],
        out_specs=pl.BlockSpec((tm, D_MODEL), lambda i, be, fi, nu: (i, 0)),
        scratch_shapes=[pltpu.VMEM((D_MODEL, 2 * D_EXPERT), BF16), pltpu.VMEM((D_EXPERT, D_MODEL), BF16)],
    )
    return pl.pallas_call(
        _moe_body,
        grid_spec=grid_spec,
        out_shape=jax.ShapeDtypeStruct((n_slots, D_MODEL), F32),
        compiler_params=_cparams(("arbitrary",)),
        name="moe_experts",
    )(blk_expert, blk_first, n_used, xs, slot_gate,
      w1, b1.reshape(N_EXPERTS, 1, 2 * D_EXPERT), w2, b2.reshape(N_EXPERTS, 1, D_MODEL))


def _moe(h2, top_i, top_g, w1, b1, w2, b2):
    tm = TM_MOE
    T = h2.shape[0]
    A = T * TOP_K
    e_flat = top_i.reshape(-1)
    order = jnp.argsort(e_flat, stable=True).astype(jnp.int32)
    inv = jnp.zeros((A,), jnp.int32).at[order].set(jnp.arange(A, dtype=jnp.int32))
    counts = jnp.sum((e_flat[:, None] == jnp.arange(N_EXPERTS, dtype=jnp.int32)[None, :]).astype(jnp.int32), axis=0)
    padded = ((counts + tm - 1) // tm) * tm
    start = jnp.cumsum(counts) - counts
    pend = jnp.cumsum(padded)
    pstart = pend - padded
    n_slots = A + N_EXPERTS * tm
    nblk = n_slots // tm
    blk_expert = jnp.minimum(jnp.searchsorted(pend, jnp.arange(nblk, dtype=jnp.int32) * tm, side="right"),
                             N_EXPERTS - 1).astype(jnp.int32)
    blk_first = jnp.concatenate([jnp.ones((1,), jnp.int32),
                                 (blk_expert[1:] != blk_expert[:-1]).astype(jnp.int32)])
    n_used = (pend[-1:] // tm).astype(jnp.int32)
    slot = jnp.arange(n_slots, dtype=jnp.int32)
    e_s = blk_expert[slot // tm]
    j = slot - pstart[e_s]
    valid = (j < counts[e_s]) & (slot < pend[-1])
    src = order[jnp.clip(start[e_s] + j, 0, A - 1)]
    slot_tok = jnp.where(valid, src // TOP_K, T)
    slot_gate = jnp.where(valid, top_g.reshape(-1)[src], 0.0)
    dest = pstart[e_flat] + (inv - start[e_flat])
    xs = jnp.concatenate([h2, jnp.zeros((1, D_MODEL), h2.dtype)], axis=0)[slot_tok]
    ys = _moe_experts(xs, slot_gate[:, None], blk_expert, blk_first, n_used, w1, b1, w2, b2)
    return jnp.sum(ys[dest.reshape(T, TOP_K)], axis=1)


def _resid_body(x_ref, f_ref, g_ref, np_ref, o_ref):
    o_ref[...] = x_ref[...] + g_ref[0] * _rms(f_ref[...], np_ref[...])


def _residual(x1, f, g2, npost1):
    tm = 512
    tok = pl.BlockSpec((tm, D_MODEL), lambda i: (i, 0))
    return pl.pallas_call(
        _resid_body,
        grid=(T_ALL // tm,),
        in_specs=[tok, tok, _seg_spec(tm), _const_spec((1, D_MODEL))],
        out_specs=tok,
        out_shape=jax.ShapeDtypeStruct((T_ALL, D_MODEL), F32),
        compiler_params=_cparams(("arbitrary",)),
        name="moe_residual",
    )(x1, f, g2, npost1)


def kernel(x_prompt, x_sample, cache_k, cache_v, state_ssm, c, c_ctx, w_mod, b_mod, norm_pre, norm_post, w_in, hy_short_w, hy_short_b, hy_f_w1, hy_f_b1, hy_f_freq, hy_f_w2, hy_f_b2, hy_f_w3, hy_decay, hy_bias, attn_sink, s5_a_re, s5_a_im, s5_log_dt, s5_b_re, s5_b_im, s5_c_re, s5_c_im, s5_d, s5_glu_w, s5_glu_b, w_br_h, w_br_a, w_br_s, w_out, router_w, router_b, exp_w1, exp_b1, exp_w2, exp_b2):
    x = jnp.concatenate([x_prompt.reshape(T_CTX, D_MODEL), x_sample.reshape(T_LAT, D_MODEL)], axis=0)
    cond8 = jnp.concatenate([c_ctx[None, :], c, jnp.zeros((8 - 1 - DEC_BATCH, D_MODEL), F32)], axis=0)
    mod_all = _modulation(cond8, w_mod, b_mod)
    rope_tabs = _rope_tables()
    new_k, new_v, new_s = [], [], []
    for l in range(DEPTH):
        mods = [mod_all[l, :, j * D_MODEL:(j + 1) * D_MODEL].reshape(8, 1, D_MODEL) for j in range(6)]
        sh1, sc1, g1, sh2, sc2, g2 = mods
        w_in_b = w_in[l].astype(BF16)
        hy_in, q, k, v, s_in = _in_proj(x, norm_pre[l, 0][None], sh1, sc1, w_in_b[:, :MIX_WIDTH], rope_tabs)
        new_k.append(k[:T_CTX].reshape(BATCH, SEQ, N_KV_HEADS, HEAD_DIM))
        new_v.append(v[:T_CTX].reshape(BATCH, SEQ, N_KV_HEADS, HEAD_DIM))
        att = _attention(q, k, v, cache_k[:, l].reshape(DEC_BATCH, PAST_LEN, KV_WIDTH),
                         cache_v[:, l].reshape(DEC_BATCH, PAST_LEN, KV_WIDTH), attn_sink[l])
        p = dict(hy_short_w=hy_short_w[l], hy_short_b=hy_short_b[l], hy_f_w1=hy_f_w1[l], hy_f_b1=hy_f_b1[l],
                 hy_f_freq=hy_f_freq[l], hy_f_w2=hy_f_w2[l], hy_f_b2=hy_f_b2[l], hy_f_w3=hy_f_w3[l],
                 hy_decay=hy_decay[l], hy_bias=hy_bias[l], s5_a_re=s5_a_re[l], s5_a_im=s5_a_im[l],
                 s5_log_dt=s5_log_dt[l], s5_b_re=s5_b_re[l], s5_b_im=s5_b_im[l], s5_c_re=s5_c_re[l],
                 s5_c_im=s5_c_im[l])
        hy = _hyena(hy_in, p)
        st = state_ssm[:, l]
        yf, yb, s_fin = _s5_scan(s_in, p, lax.complex(st[..., 0], st[..., 1]))
        new_s.append(jnp.stack([s_fin.real, s_fin.imag], axis=-1))
        vecs = [norm_pre[l, 0][None], norm_post[l, 0][None], norm_pre[l, 1][None], s5_d[l][None]]
        x1, h2, top_i, top_g = _merge(
            x, hy, att, yf, yb, s_in, [sh1, sc1, g1, sh2, sc2],
            vecs[:3] + [vecs[3], ],
            [s5_glu_w[l].astype(BF16), s5_glu_b[l][None], w_in_b[:, MIX_WIDTH:], w_br_h[l].astype(BF16),
             w_br_a[l].astype(BF16), w_br_s[l].astype(BF16), w_out[l].astype(BF16), router_w[l],
             router_b[l][None]])
        f = _moe(h2, top_i, top_g, exp_w1[l], exp_b1[l], exp_w2[l], exp_b2[l])
        x = _residual(x1, f, g2, norm_post[l, 1][None])
    y_prompt = x[:T_CTX].reshape(BATCH, SEQ, D_MODEL)
    y_sample = x[T_CTX:].reshape(DEC_BATCH, DEC_SEQ, D_MODEL)
    return (y_prompt, y_sample, jnp.stack(new_k, axis=1), jnp.stack(new_v, axis=1), jnp.stack(new_s, axis=1))
```

```python
import functools
import math

import jax
import jax.numpy as jnp
import numpy as np
from jax import lax
from jax.experimental import pallas as pl
from jax.experimental.pallas import tpu as pltpu

F32 = jnp.float32
BF16 = jnp.bfloat16

D_MODEL = 1024
BATCH = 16
SEQ = 256
DEPTH = 2
DEC_BATCH = 4
DEC_SEQ = 4096
PAST_LEN = 256
GRID_W = 64
HY_WIDTH = D_MODEL // 4
HEAD_DIM = 64
N_HEADS = (D_MODEL // 2) // HEAD_DIM
N_KV_HEADS = N_HEADS // 4
GQ = N_HEADS // N_KV_HEADS
ATT_WIDTH = N_HEADS * HEAD_DIM
KV_WIDTH = N_KV_HEADS * HEAD_DIM
S5_WIDTH = D_MODEL // 4
S5_GROUP = 16
S5_GROUPS = S5_WIDTH // S5_GROUP
S5_STATE = 64
N_BRANCH = 3
HY_ORDER = 2
HY_BANDS = 16
WINDOW = 128
Q_BLOCK = 128
ROPE_BASE = 10000.0
NEG_INF = -1e30
N_EXPERTS = 32
TOP_K = 4
D_EXPERT = D_MODEL
SWIGLU_LIMIT = 7.0
SWIGLU_ALPHA = 1.702
RMS_EPS = 1e-6

T_CTX = BATCH * SEQ
T_LAT = DEC_BATCH * DEC_SEQ
T_ALL = T_CTX + T_LAT
SEG = 4096
N_SEG = T_ALL // SEG
assert T_CTX == SEG and DEC_SEQ == SEG

O_HY = 3 * HY_WIDTH
O_Q = O_HY + ATT_WIDTH
O_K = O_Q + KV_WIDTH
O_V = O_K + KV_WIDTH
O_S = O_V + S5_WIDTH
MIX_WIDTH = O_S

VMEM_LIMIT = 56 * 1024 * 1024

TM_PROJ = 512
TM_MERGE = 256
TM_MOE = 512


def _cparams(sem, vmem=VMEM_LIMIT):
    return pltpu.CompilerParams(dimension_semantics=sem, vmem_limit_bytes=vmem)


def _rms(x, g):
    return x * lax.rsqrt(jnp.mean(x * x, axis=-1, keepdims=True) + RMS_EPS) * g


def _mod_body(c_ref, w_ref, b_ref, o_ref):
    c = c_ref[...]
    a = c * jax.nn.sigmoid(c)
    o_ref[0] = jnp.dot(a, w_ref[0], preferred_element_type=F32,
                       precision=lax.Precision.HIGHEST) + b_ref[0]


def _modulation(cond8, w_mod, b_mod):
    tn = 1536
    n = 6 * D_MODEL
    return pl.pallas_call(
        _mod_body,
        grid=(DEPTH, n // tn),
        in_specs=[pl.BlockSpec((8, D_MODEL), lambda l, j: (0, 0)),
                  pl.BlockSpec((1, D_MODEL, tn), lambda l, j: (l, 0, j)),
                  pl.BlockSpec((1, 1, tn), lambda l, j: (l, 0, j))],
        out_specs=pl.BlockSpec((1, 8, tn), lambda l, j: (l, 0, j)),
        out_shape=jax.ShapeDtypeStruct((DEPTH, 8, n), F32),
        compiler_params=_cparams(("arbitrary", "arbitrary")),
        name="modulation",
    )(cond8, w_mod, b_mod.reshape(DEPTH, 1, n))


def _rope_tables():
    t = np.arange(SEG)
    row = (t // GRID_W).astype(np.float64)
    col = (t % GRID_W).astype(np.float64)
    quarter = HEAD_DIM // 4
    inv = ROPE_BASE ** (-np.arange(quarter, dtype=np.float64) / quarter)
    lane = np.arange(128)
    d = lane % HEAD_DIM
    pos = np.where((d // 32)[None, :] == 0, row[:, None], col[:, None])
    ang = pos * inv[d % quarter][None, :]
    first = ((d % 32) < quarter)[None, :]
    cos = np.cos(ang)
    sin = np.sin(ang)
    sin_a = np.where(first, -sin, 0.0)
    sin_b = np.where(first, 0.0, sin)
    ident = np.zeros((SEG, 128))
    tab = lambda ctx, lat: jnp.asarray(np.concatenate([ctx, lat], axis=0), F32)
    return tab(ident + 1.0, cos), tab(ident, sin_a), tab(ident, sin_b)


def _rope(x, cos, sin_a, sin_b):
    w = x.shape[-1]
    rep = w // 128
    if rep > 1:
        cos = jnp.concatenate([cos] * rep, axis=1)
        sin_a = jnp.concatenate([sin_a] * rep, axis=1)
        sin_b = jnp.concatenate([sin_b] * rep, axis=1)
    quarter = HEAD_DIM // 4
    return x * cos + pltpu.roll(x, w - quarter, 1) * sin_a + pltpu.roll(x, quarter, 1) * sin_b


def _proj_body(x_ref, g_ref, sh_ref, sc_ref, w_ref, cos_ref, sa_ref, sb_ref,
               hy_ref, q_ref, k_ref, v_ref, s_ref):
    h = _rms(x_ref[...], g_ref[...]) * (1.0 + sc_ref[0]) + sh_ref[0]
    p = jnp.dot(h.astype(BF16), w_ref[...], preferred_element_type=F32)
    cos, sa, sb = cos_ref[...], sa_ref[...], sb_ref[...]
    hy_ref[...] = p[:, :O_HY]
    q_ref[...] = _rope(p[:, O_HY:O_Q], cos, sa, sb)
    k_ref[...] = _rope(p[:, O_Q:O_K], cos, sa, sb)
    v_ref[...] = p[:, O_K:O_V]
    s_ref[...] = p[:, O_V:O_S]


def _seg_spec(tm):
    return pl.BlockSpec((1, 1, D_MODEL), lambda i: ((i * tm) // SEG, 0, 0))


def _const_spec(shape):
    nd = len(shape)
    return pl.BlockSpec(shape, lambda i: (0,) * nd, pipeline_mode=pl.Buffered(1))


def _in_proj(x, gain, sh, sc, w_mix, rope_tabs):
    tm = TM_PROJ
    per_seg = SEG // tm
    rope_spec = pl.BlockSpec((tm, 128), lambda i: (jnp.where(i < per_seg, i, per_seg + i % per_seg), 0))
    tok = lambda w: pl.BlockSpec((tm, w), lambda i: (i, 0))
    widths = (O_HY, ATT_WIDTH, KV_WIDTH, KV_WIDTH, S5_WIDTH)
    return pl.pallas_call(
        _proj_body,
        grid=(T_ALL // tm,),
        in_specs=[tok(D_MODEL), _const_spec((1, D_MODEL)), _seg_spec(tm), _seg_spec(tm),
                  _const_spec((D_MODEL, MIX_WIDTH)), rope_spec, rope_spec, rope_spec],
        out_specs=[tok(w) for w in widths],
        out_shape=[jax.ShapeDtypeStruct((T_ALL, w), F32) for w in widths],
        compiler_params=_cparams(("arbitrary",)),
        name="in_proj",
    )(x, gain, sh, sc, w_mix, *rope_tabs)


def _attend(q, keys, vals, masks, sink_ref):
    qb = q.shape[0]
    scale = HEAD_DIM ** -0.5
    grp = lax.broadcasted_iota(jnp.int32, (GQ * qb, 1), 0) // qb
    outs = []
    for h in range(N_KV_HEADS):
        lo = h * HEAD_DIM
        qs = jnp.concatenate([q[:, (h * GQ + g) * HEAD_DIM:(h * GQ + g + 1) * HEAD_DIM]
                              for g in range(GQ)], axis=0)
        qs = (qs * scale).astype(BF16)
        sink = jnp.zeros((GQ * qb, 1), F32)
        for g in range(GQ):
            sink = jnp.where(grp == g, sink_ref[h * GQ + g], sink)
        logits = []
        for k_i, m_i in zip(keys, masks):
            s = lax.dot_general(qs, k_i[:, lo:lo + HEAD_DIM].astype(BF16),
                                (((1,), (1,)), ((), ())), preferred_element_type=F32)
            if m_i is not None:
                s = jnp.where(jnp.concatenate([m_i] * GQ, axis=0), s, NEG_INF)
            logits.append(s)
        m = sink
        for s in logits:
            m = jnp.maximum(m, jnp.max(s, axis=-1, keepdims=True))
        denom = jnp.exp(sink - m)
        acc = jnp.zeros((GQ * qb, HEAD_DIM), F32)
        for s, v_i in zip(logits, vals):
            p = jnp.exp(s - m)
            denom = denom + jnp.sum(p, axis=-1, keepdims=True)
            acc = acc + jnp.dot(p.astype(BF16), v_i[:, lo:lo + HEAD_DIM].astype(BF16),
                                preferred_element_type=F32)
        o = acc / denom
        outs.extend(o[g * qb:(g + 1) * qb] for g in range(GQ))
    return jnp.concatenate(outs, axis=1)


def _ctx_attn_body(sink_ref, q_ref, k_ref, v_ref, o_ref):
    o_ref[0] = _attend(q_ref[0], [k_ref[0]], [v_ref[0]], [None], sink_ref)


def _lat_attn_body(sink_ref, q_ref, kp_ref, kc_ref, kn_ref, vp_ref, vc_ref, vn_ref, ck_ref, cv_ref, o_ref):
    i = pl.program_id(1)
    kw = jnp.concatenate([kp_ref[0], kc_ref[0], kn_ref[0]], axis=0)
    vw = jnp.concatenate([vp_ref[0], vc_ref[0], vn_ref[0]], axis=0)
    qpos = i * Q_BLOCK + lax.broadcasted_iota(jnp.int32, (Q_BLOCK, 3 * Q_BLOCK), 0)
    kpos = (i - 1) * Q_BLOCK + lax.broadcasted_iota(jnp.int32, (Q_BLOCK, 3 * Q_BLOCK), 1)
    band = (jnp.abs(qpos - kpos) <= WINDOW) & (kpos >= 0) & (kpos < DEC_SEQ)
    o_ref[0] = _attend(q_ref[0], [kw, ck_ref[0]], [vw, cv_ref[0]], [band, None], sink_ref)


def _attention(q, k, v, ck, cv, sink):
    smem = pl.BlockSpec(memory_space=pltpu.SMEM)
    qc = q[:T_CTX].reshape(BATCH, SEQ, ATT_WIDTH)
    kc = k[:T_CTX].reshape(BATCH, SEQ, KV_WIDTH)
    vc = v[:T_CTX].reshape(BATCH, SEQ, KV_WIDTH)
    kv_spec = pl.BlockSpec((1, SEQ, KV_WIDTH), lambda b, i: (b, 0, 0))
    att_c = pl.pallas_call(
        _ctx_attn_body,
        grid=(BATCH, SEQ // Q_BLOCK),
        in_specs=[smem, pl.BlockSpec((1, Q_BLOCK, ATT_WIDTH), lambda b, i: (b, i, 0)), kv_spec, kv_spec],
        out_specs=pl.BlockSpec((1, Q_BLOCK, ATT_WIDTH), lambda b, i: (b, i, 0)),
        out_shape=jax.ShapeDtypeStruct((BATCH, SEQ, ATT_WIDTH), F32),
        compiler_params=_cparams(("arbitrary", "arbitrary")),
        name="ctx_attention",
    )(sink, qc, kc, vc)
    nb = DEC_SEQ // Q_BLOCK
    ql = q[T_CTX:].reshape(DEC_BATCH, DEC_SEQ, ATT_WIDTH)
    kl = k[T_CTX:].reshape(DEC_BATCH, DEC_SEQ, KV_WIDTH)
    vl = v[T_CTX:].reshape(DEC_BATCH, DEC_SEQ, KV_WIDTH)
    blk = lambda f: pl.BlockSpec((1, Q_BLOCK, KV_WIDTH), lambda b, i: (b, f(i), 0))
    prev, cur, nxt = blk(lambda i: jnp.maximum(i - 1, 0)), blk(lambda i: i), blk(lambda i: jnp.minimum(i + 1, nb - 1))
    cache_spec = pl.BlockSpec((1, PAST_LEN, KV_WIDTH), lambda b, i: (b, 0, 0))
    att_l = pl.pallas_call(
        _lat_attn_body,
        grid=(DEC_BATCH, nb),
        in_specs=[smem, pl.BlockSpec((1, Q_BLOCK, ATT_WIDTH), lambda b, i: (b, i, 0)),
                  prev, cur, nxt, prev, cur, nxt, cache_spec, cache_spec],
        out_specs=pl.BlockSpec((1, Q_BLOCK, ATT_WIDTH), lambda b, i: (b, i, 0)),
        out_shape=jax.ShapeDtypeStruct((DEC_BATCH, DEC_SEQ, ATT_WIDTH), F32),
        compiler_params=_cparams(("arbitrary", "arbitrary")),
        name="lat_attention",
    )(sink, ql, kl, kl, kl, vl, vl, vl, ck, cv)
    return jnp.concatenate([att_c.reshape(T_CTX, ATT_WIDTH), att_l.reshape(T_LAT, ATT_WIDTH)], axis=0)


def _short_conv(x, w, b):
    xp = jnp.pad(x, ((0, 0), (1, 1), (0, 0)))
    return xp[:, :-2] * w[0] + xp[:, 1:-1] * w[1] + xp[:, 2:] * w[2] + b


def _hyena_filters(L, p):
    t = jnp.arange(L, dtype=F32)
    t_norm = t / max(L - 1, 1)
    bands = jnp.arange(1, HY_BANDS + 1, dtype=F32)
    ang = (2.0 * math.pi / L) * t[:, None] * bands[None, :]
    z = jnp.concatenate([t_norm[:, None], jnp.cos(ang), jnp.sin(ang)], axis=-1)
    h = jnp.sin(p["hy_f_freq"][0] * (z @ p["hy_f_w1"] + p["hy_f_b1"]))
    h = jnp.sin(p["hy_f_freq"][1] * (h @ p["hy_f_w2"] + p["hy_f_b2"]))
    h = (h @ p["hy_f_w3"]).astype(F32).reshape(L, HY_ORDER, 2, HY_WIDTH)
    h = h * jnp.exp(-t_norm[:, None, None, None] * jnp.abs(p["hy_decay"].astype(F32))[None])
    return h / (jnp.sum(jnp.abs(h), axis=(0, 2), keepdims=True) + 1e-6)


def _short_conv_body(x_ref, w_ref, b_ref, o_ref):
    x = x_ref[0]
    L = x.shape[0]
    t = lax.broadcasted_iota(jnp.int32, x.shape, 0)
    prev = jnp.where(t == 0, 0.0, pltpu.roll(x, 1, 0))
    nxt = jnp.where(t == L - 1, 0.0, pltpu.roll(x, L - 1, 0))
    o_ref[0] = prev * w_ref[0:1, :] + x * w_ref[1:2, :] + nxt * w_ref[2:3, :] + b_ref[...]


def _short_conv_tm(x, w, b):
    nb, L, _ = x.shape
    nct = O_HY // 128
    per = HY_WIDTH // 128
    return pl.pallas_call(
        _short_conv_body,
        grid=(nb, nct),
        in_specs=[pl.BlockSpec((1, L, 128), lambda bi, j: (bi, 0, j)),
                  pl.BlockSpec((3, 128), lambda bi, j: (0, j)),
                  pl.BlockSpec((1, 128), lambda bi, j: (0, j))],
        out_specs=pl.BlockSpec((1, L, 128), lambda bi, j: (j // per, 0, bi * per + j % per)),
        out_shape=jax.ShapeDtypeStruct((3, L, nb * HY_WIDTH), F32),
        compiler_params=_cparams(("arbitrary", "arbitrary")),
        name="hyena_short_conv",
    )(x, w, b[None])


class _FftPlan:
    def __init__(self, L, n1, n2):
        N = 2 * L
        assert n1 * n2 == N
        self.L, self.N, self.n1, self.n2 = L, N, n1, n2
        self.kin = n1 // 2
        k1 = np.arange(n1)
        th = 2.0 * np.pi * np.outer(k1, np.arange(n1)) / n1
        f_re, f_im = np.cos(th), -np.sin(th)
        self._mats = {"f_full": np.concatenate([f_re, f_im], axis=0),
                      "f_half": np.concatenate([f_re, f_im], axis=0)[:, :self.kin],
                      "f_inv": np.concatenate([f_re.T, f_im.T], axis=1)[:self.kin]}
        if n2 > 1:
            k2 = np.arange(n2)
            idx = (np.outer(k2, k2)[None] * n1 + k1[:, None, None] * k2[None, None, :]) % N
            ph = 2.0 * np.pi * idx / N
            g_re, g_im = np.cos(ph), -np.sin(ph)
            blk = lambda a, b: np.concatenate([np.concatenate([a, -b], axis=2),
                                               np.concatenate([b, a], axis=2)], axis=1)
            self._mats["g"] = blk(g_re, g_im)
            h_re, h_im = np.swapaxes(g_re, 1, 2), -np.swapaxes(g_im, 1, 2)
            self._mats["gh"] = blk(h_re, h_im)

    def __getattr__(self, name):
        mats = self.__dict__.get("_mats", {})
        if name in mats:
            return jnp.asarray(mats[name].astype(np.float32), BF16)
        raise AttributeError(name)


def _fft_s1_body(x_ref, f_ref, o_ref):
    o_ref[...] = jnp.dot(f_ref[...], x_ref[0].astype(BF16), preferred_element_type=F32).astype(o_ref.dtype)


def _fft_s1(x3, plane, f, out_dtype=BF16):
    _, kin, m = x3.shape
    rows = f.shape[0]
    tn = min(m, (4 * 1024 * 1024) // (rows * 2))
    return pl.pallas_call(
        _fft_s1_body,
        grid=(m // tn,),
        in_specs=[pl.BlockSpec((1, kin, tn), lambda j: (plane, 0, j)), pl.BlockSpec(f.shape, lambda j: (0, 0))],
        out_specs=pl.BlockSpec((rows, tn), lambda j: (0, j)),
        out_shape=jax.ShapeDtypeStruct((rows, m), out_dtype),
        compiler_params=_cparams(("arbitrary",)),
        name="fft_stage1",
    )(x3, f)


def _cmul_tiled(x_re, x_im, k_re, k_im):
    rep = x_re.shape[-1] // k_re.shape[-1]
    if rep > 1:
        k_re = jnp.concatenate([k_re] * rep, axis=-1)
        k_im = jnp.concatenate([k_im] * rep, axis=-1)
    return x_re * k_re - x_im * k_im, x_re * k_im + x_im * k_re


def _fft_mid_body(a_ref, g_ref, gh_ref, kf_ref, o_ref, *, kb, n2):
    for j in range(kb):
        a = jnp.concatenate([a_ref[0, j], a_ref[1, j]], axis=0)
        x = jnp.dot(g_ref[j], a, preferred_element_type=F32)
        y_re, y_im = _cmul_tiled(x[:n2], x[n2:], kf_ref[0, j], kf_ref[1, j])
        y = jnp.concatenate([y_re, y_im], axis=0).astype(BF16)
        b = jnp.dot(gh_ref[j], y, preferred_element_type=F32)
        o_ref[0, j] = b[:n2].astype(o_ref.dtype)
        o_ref[1, j] = b[n2:].astype(o_ref.dtype)


def _fft_fwd2_body(a_ref, g_ref, o_ref, *, kb, n2):
    for j in range(kb):
        a = jnp.concatenate([a_ref[0, j], a_ref[1, j]], axis=0)
        x = jnp.dot(g_ref[j], a, preferred_element_type=F32)
        o_ref[0, j] = x[:n2]
        o_ref[1, j] = x[n2:]


def _fft_mid(plan, a, kf=None, order=0):
    n1, n2 = plan.n1, plan.n2
    w = a.shape[1] // n2
    a4 = a.reshape(2, n1, n2, w)
    kb = 8
    a_spec = pl.BlockSpec((2, kb, n2, w), lambda i: (0, i, 0, 0))
    g_spec = pl.BlockSpec((kb, 2 * n2, 2 * n2), lambda i: (i, 0, 0))
    if kf is None:
        return pl.pallas_call(
            functools.partial(_fft_fwd2_body, kb=kb, n2=n2),
            grid=(n1 // kb,),
            in_specs=[a_spec, g_spec],
            out_specs=a_spec,
            out_shape=jax.ShapeDtypeStruct((2, n1, n2, w), F32),
            compiler_params=_cparams(("arbitrary",)),
            name="fft_stage2",
        )(a4, plan.g)
    out = pl.pallas_call(
        functools.partial(_fft_mid_body, kb=kb, n2=n2),
        grid=(n1 // kb,),
        in_specs=[a_spec, g_spec, g_spec,
                  pl.BlockSpec((2, kb, n2, HY_WIDTH), lambda i: (0, i, 0, order))],
        out_specs=a_spec,
        out_shape=jax.ShapeDtypeStruct((2, n1, n2, w), BF16),
        compiler_params=_cparams(("arbitrary",)),
        name="fft_mid",
    )(a4, plan.g, plan.gh, kf)
    return out.reshape(2 * n1, n2 * w)


def _fft_inv_body(*refs, n1, inv_n, spectral, chain):
    b_ref, fi_ref, u_ref, gate_ref, bias_ref = refs[:5]
    rest = list(refs[5:])
    bm = b_ref[...]
    if spectral:
        kf_ref = rest.pop(0)
        a = bm.astype(F32)
        y_re, y_im = _cmul_tiled(a[:n1], a[n1:], kf_ref[:n1, :], kf_ref[n1:, :])
        bm = jnp.concatenate([y_re, y_im], axis=0).astype(BF16)
    y = jnp.dot(fi_ref[...], bm, preferred_element_type=F32) * inv_n
    u = u_ref[0]
    z = gate_ref[0] * (y + u * bias_ref[...])
    if chain:
        fs_ref, z_ref, a2_ref = rest
        z_ref[0] = z
        a2_ref[...] = jnp.dot(fs_ref[...], z.astype(BF16), preferred_element_type=F32).astype(BF16)
    else:
        (z_ref,) = rest
        z_ref[0] = z


def _fft_inv(plan, bm, u3, u_plane, gate3, gate_plane, bias_t, kf=None, order=0, chain=False):
    kin, n1 = plan.kin, plan.n1
    m = bm.shape[1]
    tn = bias_t.shape[1]
    in_specs = [pl.BlockSpec((2 * n1, tn), lambda j: (0, j)),
                pl.BlockSpec((kin, 2 * n1), lambda j: (0, 0)),
                pl.BlockSpec((1, kin, tn), lambda j: (u_plane, 0, j)),
                pl.BlockSpec((1, kin, tn), lambda j: (gate_plane, 0, j)),
                pl.BlockSpec((1, tn), lambda j: (0, 0))]
    args = [bm, plan.f_inv, u3, gate3, bias_t]
    if kf is not None:
        in_specs.append(pl.BlockSpec((2 * n1, HY_WIDTH), lambda j: (0, order)))
        args.append(kf)
    out_specs = [pl.BlockSpec((1, kin, tn), lambda j: (0, 0, j))]
    out_shape = [jax.ShapeDtypeStruct((1, kin, m), F32)]
    if chain:
        in_specs.append(pl.BlockSpec((2 * n1, kin), lambda j: (0, 0)))
        args.append(plan.f_half)
        out_specs.append(pl.BlockSpec((2 * n1, tn), lambda j: (0, j)))
        out_shape.append(jax.ShapeDtypeStruct((2 * n1, m), BF16))
    return pl.pallas_call(
        functools.partial(_fft_inv_body, n1=n1, inv_n=1.0 / plan.N, spectral=kf is not None, chain=chain),
        grid=(m // tn,),
        in_specs=in_specs,
        out_specs=out_specs,
        out_shape=out_shape,
        compiler_params=_cparams(("arbitrary",)),
        name="fft_inverse",
    )(*args)


_PLAN_CTX = _FftPlan(SEQ, 2 * SEQ, 1)
_PLAN_LAT = _FftPlan(DEC_SEQ, 128, 64)


def _hyena_group(x, p, plan):
    nb, L, _ = x.shape
    w = nb * HY_WIDTH
    kin, n2 = plan.kin, plan.n2
    m = n2 * w
    sc = _short_conv_tm(x, p["hy_short_w"], p["hy_short_b"]).reshape(3, kin, m)
    h = _hyena_filters(L, p)
    kk = jnp.concatenate([h[:, :, 0], jnp.zeros((1, HY_ORDER, HY_WIDTH), F32), h[:0:-1, :, 1]], axis=0)
    kk = kk.reshape(1, plan.n1, n2 * HY_ORDER * HY_WIDTH)
    kf = _fft_s1(kk, 0, plan.f_full, out_dtype=BF16 if n2 > 1 else F32)
    tn = min(m, 2048 if n2 == 1 else 8192)
    bias_t = [jnp.tile(p["hy_bias"][o], tn // HY_WIDTH)[None] for o in range(HY_ORDER)]
    a = _fft_s1(sc, 2, plan.f_half)
    if n2 > 1:
        kf = _fft_mid(plan, kf)
        bm = _fft_mid(plan, a, kf, 0)
        z, a2 = _fft_inv(plan, bm, sc, 2, sc, 0, bias_t[0], chain=True)
        bm2 = _fft_mid(plan, a2, kf, 1)
        (hy,) = _fft_inv(plan, bm2, z, 0, sc, 1, bias_t[1])
    else:
        z, a2 = _fft_inv(plan, a, sc, 2, sc, 0, bias_t[0], kf=kf, order=0, chain=True)
        (hy,) = _fft_inv(plan, a2, z, 0, sc, 1, bias_t[1], kf=kf, order=1)
    return jnp.transpose(hy.reshape(L, nb, HY_WIDTH), (1, 0, 2))


def _hyena(hy_in, p):
    zc = _hyena_group(hy_in[:T_CTX].reshape(BATCH, SEQ, O_HY), p, _PLAN_CTX)
    zl = _hyena_group(hy_in[T_CTX:].reshape(DEC_BATCH, DEC_SEQ, O_HY), p, _PLAN_LAT)
    return jnp.concatenate([zc.reshape(T_CTX, HY_WIDTH), zl.reshape(T_LAT, HY_WIDTH)], axis=0)


S5_LANES = S5_GROUPS * S5_STATE
S5_CHUNK_ROWS = 1024


def _s5_body(u_ref, h0_ref, are_ref, aim_ref, b_ref, c_ref, y_ref, hfin_ref, s_ref, *, rows, tc):
    i = pl.program_id(0)

    @pl.when(i == 0)
    def _():
        s_ref[pl.ds(0, rows), :] = h0_ref[...]

    s_ref[pl.ds(rows, tc * rows), :] = jnp.dot(u_ref[...], b_ref[...], preferred_element_type=F32)
    re = pl.ds(0, S5_LANES)
    im = pl.ds(S5_LANES, S5_LANES)

    def step(t, carry):
        for r in range(rows // 8):
            src = pl.multiple_of(t * rows + r * 8, 8)
            dst = pl.multiple_of(src + rows, 8)
            a_re = are_ref[pl.ds(r * 8, 8), :]
            a_im = aim_ref[pl.ds(r * 8, 8), :]
            c_re = s_ref[pl.ds(src, 8), re]
            c_im = s_ref[pl.ds(src, 8), im]
            s_ref[pl.ds(dst, 8), re] = a_re * c_re - a_im * c_im + s_ref[pl.ds(dst, 8), re]
            s_ref[pl.ds(dst, 8), im] = a_re * c_im + a_im * c_re + s_ref[pl.ds(dst, 8), im]
        return carry

    lax.fori_loop(0, tc, step, 0)
    last = s_ref[pl.ds(tc * rows, rows), :]
    hfin_ref[...] = last
    y_ref[...] = jnp.dot(s_ref[pl.ds(rows, tc * rows), :].astype(BF16), c_ref[...],
                         preferred_element_type=F32)
    s_ref[pl.ds(0, rows), :] = last


def _s5_params(p):
    a_re = jnp.minimum(p["s5_a_re"], -1e-4)
    a_im = p["s5_a_im"]
    dt = jnp.exp(p["s5_log_dt"])[..., None]
    mag = jnp.exp(a_re * dt)
    ab_re, ab_im = mag * jnp.cos(a_im * dt), mag * jnp.sin(a_im * dt)
    den = a_re * a_re + a_im * a_im
    co_re = ((ab_re - 1.0) * a_re + ab_im * a_im) / den
    co_im = (ab_im * a_re - (ab_re - 1.0) * a_im) / den
    bb_re = co_re[..., None] * p["s5_b_re"] - co_im[..., None] * p["s5_b_im"]
    bb_im = co_re[..., None] * p["s5_b_im"] + co_im[..., None] * p["s5_b_re"]
    eye = jnp.eye(S5_GROUPS, dtype=F32)
    blk_in = lambda m: jnp.einsum("dgnp,gh->dgphn", m, eye).reshape(2 * S5_WIDTH, S5_LANES)
    b_cat = jnp.concatenate([blk_in(bb_re), blk_in(bb_im)], axis=1)
    blk_out = lambda m: jnp.einsum("dgpn,gh->hndgp", m, eye).reshape(S5_LANES, 2 * S5_WIDTH)
    c_cat = jnp.concatenate([blk_out(p["s5_c_re"]), blk_out(-p["s5_c_im"])], axis=0)
    return (ab_re.reshape(2, S5_LANES), ab_im.reshape(2, S5_LANES), b_cat.astype(BF16), c_cat.astype(BF16))


def _s5_group(u, h0, prm):
    ab_re, ab_im, b_cat, c_cat = prm
    nb, L = u.shape[0], u.shape[1]
    rows = 2 * nb
    tc = S5_CHUNK_ROWS // rows
    ut = jnp.transpose(u, (1, 0, 2)).astype(BF16)
    z = jnp.zeros_like(ut)
    u2 = jnp.concatenate([jnp.concatenate([ut, z], axis=2),
                          jnp.concatenate([z, ut[::-1]], axis=2)], axis=1).reshape(L * rows, 2 * S5_WIDTH)
    are = jnp.repeat(ab_re, nb, axis=0)
    aim = jnp.repeat(ab_im, nb, axis=0)
    const = lambda shape: pl.BlockSpec(shape, lambda i: (0, 0))
    y, hfin = pl.pallas_call(
        functools.partial(_s5_body, rows=rows, tc=tc),
        grid=(L // tc,),
        in_specs=[pl.BlockSpec((tc * rows, 2 * S5_WIDTH), lambda i: (i, 0)),
                  const((rows, 2 * S5_LANES)), const((rows, S5_LANES)), const((rows, S5_LANES)),
                  const((2 * S5_WIDTH, 2 * S5_LANES)), const((2 * S5_LANES, 2 * S5_WIDTH))],
        out_specs=[pl.BlockSpec((tc * rows, 2 * S5_WIDTH), lambda i: (i, 0)), const((rows, 2 * S5_LANES))],
        out_shape=[jax.ShapeDtypeStruct((L * rows, 2 * S5_WIDTH), F32),
                   jax.ShapeDtypeStruct((rows, 2 * S5_LANES), F32)],
        scratch_shapes=[pltpu.VMEM(((tc + 1) * rows, 2 * S5_LANES), F32)],
        compiler_params=_cparams(("arbitrary",)),
        name="s5_scan",
    )(u2, h0, are, aim, b_cat, c_cat)
    y = y.reshape(L, rows, 2 * S5_WIDTH)
    ysum = y[:, :nb, :S5_WIDTH] + y[::-1, nb:, S5_WIDTH:]
    return jnp.transpose(ysum, (1, 0, 2)), hfin


def _s5_scan(s_in, p, st_lat):
    prm = _s5_params(p)
    h0_ctx = jnp.zeros((2 * BATCH, 2 * S5_LANES), F32)
    h0_lat = jnp.transpose(st_lat, (1, 0, 4, 2, 3)).reshape(2 * DEC_BATCH, 2 * S5_LANES)
    y_c, hfin = _s5_group(s_in[:T_CTX].reshape(BATCH, SEQ, S5_WIDTH), h0_ctx, prm)
    y_l, _ = _s5_group(s_in[T_CTX:].reshape(DEC_BATCH, DEC_SEQ, S5_WIDTH), h0_lat, prm)
    ysum = jnp.concatenate([y_c.reshape(T_CTX, S5_WIDTH), y_l.reshape(T_LAT, S5_WIDTH)], axis=0)
    new_state = jnp.transpose(hfin.reshape(2, BATCH, 2, S5_GROUPS, S5_STATE), (1, 0, 3, 4, 2))
    return ysum, new_state


def _merge_body(x_ref, hy_ref, att_ref, ysum_ref, sin_ref,
                sh1_ref, sc1_ref, g1_ref, sh2_ref, sc2_ref,
                np0_ref, npost0_ref, np1_ref, d_ref, gluw_ref, glub_ref,
                wg_ref, wh_ref, wa_ref, ws_ref, wo_ref, rw_ref, rb_ref,
                x1_ref, h2_ref, ti_ref, tg_ref):
    x = x_ref[...]
    h = (_rms(x, np0_ref[...]) * (1.0 + sc1_ref[0]) + sh1_ref[0]).astype(BF16)
    dot = functools.partial(jnp.dot, preferred_element_type=F32)
    ys = ysum_ref[...] + sin_ref[...] * d_ref[...]
    ys = jax.nn.gelu(ys)
    g = dot(ys.astype(BF16), gluw_ref[...]) + glub_ref[...]
    s = g[:, :S5_WIDTH] * jax.nn.sigmoid(g[:, S5_WIDTH:])
    merged = jax.nn.sigmoid(dot(h, wg_ref[:, :D_MODEL])) * dot(hy_ref[...].astype(BF16), wh_ref[...])
    merged += jax.nn.sigmoid(dot(h, wg_ref[:, D_MODEL:2 * D_MODEL])) * dot(att_ref[...].astype(BF16), wa_ref[...])
    merged += jax.nn.sigmoid(dot(h, wg_ref[:, 2 * D_MODEL:])) * dot(s.astype(BF16), ws_ref[...])
    m = dot(merged.astype(BF16), wo_ref[...])
    x1 = x + g1_ref[0] * _rms(m, npost0_ref[...])
    x1_ref[...] = x1
    h2 = _rms(x1, np1_ref[...]) * (1.0 + sc2_ref[0]) + sh2_ref[0]
    h2_ref[...] = h2
    logits = jnp.dot(h2, rw_ref[...], preferred_element_type=F32,
                     precision=lax.Precision.HIGHEST) + rb_ref[...]
    lane = lax.broadcasted_iota(jnp.int32, logits.shape, 1)
    vals, idxs = [], []
    for _ in range(TOP_K):
        mx = jnp.max(logits, axis=-1, keepdims=True)
        ix = jnp.min(jnp.where(logits == mx, lane, N_EXPERTS), axis=-1, keepdims=True)
        vals.append(mx)
        idxs.append(ix)
        logits = jnp.where(lane == ix, -jnp.inf, logits)
    e = [jnp.exp(v - vals[0]) for v in vals]
    tot = e[0] + e[1] + e[2] + e[3]
    ti_ref[...] = jnp.concatenate(idxs, axis=1)
    tg_ref[...] = jnp.concatenate(e, axis=1) / tot


def _merge(x, hy, att, ysum, s_in, mods, vecs, mats):
    tm = TM_MERGE
    tok = lambda w: pl.BlockSpec((tm, w), lambda i: (i, 0))
    in_specs = ([tok(D_MODEL), tok(HY_WIDTH), tok(ATT_WIDTH), tok(S5_WIDTH), tok(S5_WIDTH)]
                + [_seg_spec(tm)] * len(mods)
                + [_const_spec(v.shape) for v in vecs]
                + [_const_spec(m.shape) for m in mats])
    return pl.pallas_call(
        _merge_body,
        grid=(T_ALL // tm,),
        in_specs=in_specs,
        out_specs=[tok(D_MODEL), tok(D_MODEL), tok(TOP_K), tok(TOP_K)],
        out_shape=[jax.ShapeDtypeStruct((T_ALL, D_MODEL), F32),
                   jax.ShapeDtypeStruct((T_ALL, D_MODEL), F32),
                   jax.ShapeDtypeStruct((T_ALL, TOP_K), jnp.int32),
                   jax.ShapeDtypeStruct((T_ALL, TOP_K), F32)],
        compiler_params=_cparams(("arbitrary",)),
        name="merge_router",
    )(x, hy, att, ysum, s_in, *mods, *vecs, *mats)


def _moe_body(be_ref, first_ref, nused_ref, xs_ref, gate_ref, w1_ref, b1_ref, w2_ref, b2_ref,
              o_ref, w1b_ref, w2b_ref):
    i = pl.program_id(0)

    @pl.when(first_ref[i] == 1)
    def _():
        w1b_ref[...] = w1_ref[0].astype(BF16)
        w2b_ref[...] = w2_ref[0].astype(BF16)

    @pl.when(i < nused_ref[0])
    def _():
        h = jnp.dot(xs_ref[...].astype(BF16), w1b_ref[...], preferred_element_type=F32) + b1_ref[0]
        glu = jnp.minimum(h[:, :D_EXPERT], SWIGLU_LIMIT)
        lin = jnp.clip(h[:, D_EXPERT:], -SWIGLU_LIMIT, SWIGLU_LIMIT)
        o = glu * jax.nn.sigmoid(SWIGLU_ALPHA * glu) * (lin + 1.0)
        y = jnp.dot(o.astype(BF16), w2b_ref[...], preferred_element_type=F32) + b2_ref[0]
        o_ref[...] = y * gate_ref[...]

    @pl.when(i >= nused_ref[0])
    def _():
        o_ref[...] = jnp.zeros_like(o_ref)


def _moe_experts(xs, slot_gate, blk_expert, blk_first, n_used, w1, b1, w2, b2):
    tm = TM_MOE
    n_slots = xs.shape[0]
    grid_spec = pltpu.PrefetchScalarGridSpec(
        num_scalar_prefetch=3,
        grid=(n_slots // tm,),
        in_specs=[pl.BlockSpec((tm, D_MODEL), lambda i, be, fi, nu: (i, 0)),
                  pl.BlockSpec((tm, 1), lambda i, be, fi, nu: (i, 0)),
                  pl.BlockSpec((1, D_MODEL, 2 * D_EXPERT), lambda i, be, fi, nu: (be[i], 0, 0)),
                  pl.BlockSpec((1, 1, 2 * D_EXPERT), lambda i, be, fi, nu: (be[i], 0, 0)),
                  pl.BlockSpec((1, D_EXPERT, D_MODEL), lambda i, be, fi, nu: (be[i], 0, 0)),
                  pl.BlockSpec((1, 1, D_MODEL), lambda i, be, fi, nu: (be[i], 0, 0))],
        out_specs=pl.BlockSpec((tm, D_MODEL), lambda i, be, fi, nu: (i, 0)),
        scratch_shapes=[pltpu.VMEM((D_MODEL, 2 * D_EXPERT), BF16), pltpu.VMEM((D_EXPERT, D_MODEL), BF16)],
    )
    return pl.pallas_call(
        _moe_body,
        grid_spec=grid_spec,
        out_shape=jax.ShapeDtypeStruct((n_slots, D_MODEL), F32),
        compiler_params=_cparams(("arbitrary",)),
        name="moe_experts",
    )(blk_expert, blk_first, n_used, xs, slot_gate,
      w1, b1.reshape(N_EXPERTS, 1, 2 * D_EXPERT), w2, b2.reshape(N_EXPERTS, 1, D_MODEL))


GATHER_CHUNK = 2048


def _gather_body(idx_ref, src_ref, out_ref, sem, *, chunk):
    base = pl.program_id(0) * chunk

    def issue(j, carry):
        pltpu.make_async_copy(src_ref.at[pl.ds(idx_ref[0, 0, j], 1)], out_ref.at[pl.ds(base + j, 1)], sem).start()
        return carry

    lax.fori_loop(0, chunk, issue, 0, unroll=8)

    def drain(j, carry):
        pltpu.make_async_copy(src_ref.at[pl.ds(0, 1)], out_ref.at[pl.ds(base, 1)], sem).wait()
        return carry

    lax.fori_loop(0, chunk, drain, 0, unroll=8)


def _row_gather(src, idx):
    m = idx.shape[0]
    chunk = GATHER_CHUNK
    return pl.pallas_call(
        functools.partial(_gather_body, chunk=chunk),
        grid=(m // chunk,),
        in_specs=[pl.BlockSpec((1, 1, chunk), lambda i: (i, 0, 0), memory_space=pltpu.SMEM),
                  pl.BlockSpec(memory_space=pl.ANY)],
        out_specs=pl.BlockSpec(memory_space=pl.ANY),
        out_shape=jax.ShapeDtypeStruct((m, src.shape[1]), src.dtype),
        scratch_shapes=[pltpu.SemaphoreType.DMA(())],
        compiler_params=_cparams(("arbitrary",)),
        name="row_gather",
    )(idx.reshape(m // chunk, 1, chunk), src)


def _moe(h2, top_i, top_g, w1, b1, w2, b2):
    tm = TM_MOE
    T = h2.shape[0]
    A = T * TOP_K
    e_flat = top_i.reshape(-1)
    order = jnp.argsort(e_flat, stable=True).astype(jnp.int32)
    inv = jnp.zeros((A,), jnp.int32).at[order].set(jnp.arange(A, dtype=jnp.int32))
    counts = jnp.sum((e_flat[:, None] == jnp.arange(N_EXPERTS, dtype=jnp.int32)[None, :]).astype(jnp.int32), axis=0)
    padded = ((counts + tm - 1) // tm) * tm
    start = jnp.cumsum(counts) - counts
    pend = jnp.cumsum(padded)
    pstart = pend - padded
    n_slots = A + N_EXPERTS * tm
    nblk = n_slots // tm
    blk_expert = jnp.minimum(jnp.searchsorted(pend, jnp.arange(nblk, dtype=jnp.int32) * tm, side="right"),
                             N_EXPERTS - 1).astype(jnp.int32)
    blk_first = jnp.concatenate([jnp.ones((1,), jnp.int32),
                                 (blk_expert[1:] != blk_expert[:-1]).astype(jnp.int32)])
    n_used = (pend[-1:] // tm).astype(jnp.int32)
    slot = jnp.arange(n_slots, dtype=jnp.int32)
    e_s = blk_expert[slot // tm]
    j = slot - pstart[e_s]
    valid = (j < counts[e_s]) & (slot < pend[-1])
    src = order[jnp.clip(start[e_s] + j, 0, A - 1)]
    slot_tok = jnp.where(valid, src // TOP_K, 0)
    slot_gate = jnp.where(valid, top_g.reshape(-1)[src], 0.0)
    dest = pstart[e_flat] + (inv - start[e_flat])
    xs = _row_gather(h2, slot_tok)
    ys = _moe_experts(xs, slot_gate[:, None], blk_expert, blk_first, n_used, w1, b1, w2, b2)
    return _row_gather(ys, dest).reshape(T, TOP_K * D_MODEL)


def _resid_body(x_ref, f_ref, g_ref, np_ref, o_ref):
    f = f_ref[:, :D_MODEL]
    for k in range(1, TOP_K):
        f = f + f_ref[:, k * D_MODEL:(k + 1) * D_MODEL]
    o_ref[...] = x_ref[...] + g_ref[0] * _rms(f, np_ref[...])


def _residual(x1, f, g2, npost1):
    tm = 512
    tok = pl.BlockSpec((tm, D_MODEL), lambda i: (i, 0))
    return pl.pallas_call(
        _resid_body,
        grid=(T_ALL // tm,),
        in_specs=[tok, pl.BlockSpec((tm, TOP_K * D_MODEL), lambda i: (i, 0)), _seg_spec(tm),
                  _const_spec((1, D_MODEL))],
        out_specs=tok,
        out_shape=jax.ShapeDtypeStruct((T_ALL, D_MODEL), F32),
        compiler_params=_cparams(("arbitrary",)),
        name="moe_residual",
    )(x1, f, g2, npost1)


def kernel(x_prompt, x_sample, cache_k, cache_v, state_ssm, c, c_ctx, w_mod, b_mod, norm_pre, norm_post, w_in, hy_short_w, hy_short_b, hy_f_w1, hy_f_b1, hy_f_freq, hy_f_w2, hy_f_b2, hy_f_w3, hy_decay, hy_bias, attn_sink, s5_a_re, s5_a_im, s5_log_dt, s5_b_re, s5_b_im, s5_c_re, s5_c_im, s5_d, s5_glu_w, s5_glu_b, w_br_h, w_br_a, w_br_s, w_out, router_w, router_b, exp_w1, exp_b1, exp_w2, exp_b2):
    x = jnp.concatenate([x_prompt.reshape(T_CTX, D_MODEL), x_sample.reshape(T_LAT, D_MODEL)], axis=0)
    cond8 = jnp.concatenate([c_ctx[None, :], c, jnp.zeros((8 - 1 - DEC_BATCH, D_MODEL), F32)], axis=0)
    mod_all = _modulation(cond8, w_mod, b_mod)
    rope_tabs = _rope_tables()
    new_k, new_v, new_s = [], [], []
    for l in range(DEPTH):
        mods = [mod_all[l, :, j * D_MODEL:(j + 1) * D_MODEL].reshape(8, 1, D_MODEL) for j in range(6)]
        sh1, sc1, g1, sh2, sc2, g2 = mods
        w_in_b = w_in[l].astype(BF16)
        hy_in, q, k, v, s_in = _in_proj(x, norm_pre[l, 0][None], sh1, sc1, w_in_b[:, :MIX_WIDTH], rope_tabs)
        new_k.append(k[:T_CTX].reshape(BATCH, SEQ, N_KV_HEADS, HEAD_DIM))
        new_v.append(v[:T_CTX].reshape(BATCH, SEQ, N_KV_HEADS, HEAD_DIM))
        att = _attention(q, k, v, cache_k[:, l].reshape(DEC_BATCH, PAST_LEN, KV_WIDTH),
                         cache_v[:, l].reshape(DEC_BATCH, PAST_LEN, KV_WIDTH), attn_sink[l])
        p = dict(hy_short_w=hy_short_w[l], hy_short_b=hy_short_b[l], hy_f_w1=hy_f_w1[l], hy_f_b1=hy_f_b1[l],
                 hy_f_freq=hy_f_freq[l], hy_f_w2=hy_f_w2[l], hy_f_b2=hy_f_b2[l], hy_f_w3=hy_f_w3[l],
                 hy_decay=hy_decay[l], hy_bias=hy_bias[l], s5_a_re=s5_a_re[l], s5_a_im=s5_a_im[l],
                 s5_log_dt=s5_log_dt[l], s5_b_re=s5_b_re[l], s5_b_im=s5_b_im[l], s5_c_re=s5_c_re[l],
                 s5_c_im=s5_c_im[l])
        hy = _hyena(hy_in, p)
        ysum, s_fin = _s5_scan(s_in, p, state_ssm[:, l])
        new_s.append(s_fin)
        vecs = [norm_pre[l, 0][None], norm_post[l, 0][None], norm_pre[l, 1][None], s5_d[l][None]]
        x1, h2, top_i, top_g = _merge(
            x, hy, att, ysum, s_in, [sh1, sc1, g1, sh2, sc2], vecs,
            [s5_glu_w[l].astype(BF16), s5_glu_b[l][None], w_in_b[:, MIX_WIDTH:], w_br_h[l].astype(BF16),
             w_br_a[l].astype(BF16), w_br_s[l].astype(BF16), w_out[l].astype(BF16), router_w[l],
             router_b[l][None]])
        f = _moe(h2, top_i, top_g, exp_w1[l], exp_b1[l], exp_w2[l], exp_b2[l])
        x = _residual(x1, f, g2, norm_post[l, 1][None])
    y_prompt = x[:T_CTX].reshape(BATCH, SEQ, D_MODEL)
    y_sample = x[T_CTX:].reshape(DEC_BATCH, DEC_SEQ, D_MODEL)
    return (y_prompt, y_sample, jnp.stack(new_k, axis=1), jnp.stack(new_v, axis=1), jnp.stack(new_s, axis=1))
```

```python
import functools
import math

import jax
import jax.numpy as jnp
import numpy as np
from jax import lax
from jax.experimental import pallas as pl
from jax.experimental.pallas import tpu as pltpu

F32 = jnp.float32
BF16 = jnp.bfloat16

D_MODEL = 1024
BATCH = 16
SEQ = 256
DEPTH = 2
DEC_BATCH = 4
DEC_SEQ = 4096
PAST_LEN = 256
GRID_W = 64
HY_WIDTH = D_MODEL // 4
HEAD_DIM = 64
N_HEADS = (D_MODEL // 2) // HEAD_DIM
N_KV_HEADS = N_HEADS // 4
GQ = N_HEADS // N_KV_HEADS
ATT_WIDTH = N_HEADS * HEAD_DIM
KV_WIDTH = N_KV_HEADS * HEAD_DIM
S5_WIDTH = D_MODEL // 4
S5_GROUP = 16
S5_GROUPS = S5_WIDTH // S5_GROUP
S5_STATE = 64
N_BRANCH = 3
HY_ORDER = 2
HY_BANDS = 16
WINDOW = 128
Q_BLOCK = 128
ROPE_BASE = 10000.0
NEG_INF = -1e30
N_EXPERTS = 32
TOP_K = 4
D_EXPERT = D_MODEL
SWIGLU_LIMIT = 7.0
SWIGLU_ALPHA = 1.702
RMS_EPS = 1e-6

T_CTX = BATCH * SEQ
T_LAT = DEC_BATCH * DEC_SEQ
T_ALL = T_CTX + T_LAT
SEG = 4096
N_SEG = T_ALL // SEG
assert T_CTX == SEG and DEC_SEQ == SEG

O_HY = 3 * HY_WIDTH
O_Q = O_HY + ATT_WIDTH
O_K = O_Q + KV_WIDTH
O_V = O_K + KV_WIDTH
O_S = O_V + S5_WIDTH
MIX_WIDTH = O_S

VMEM_LIMIT = 56 * 1024 * 1024

TM_MOE = 512


def _cparams(sem, vmem=VMEM_LIMIT):
    return pltpu.CompilerParams(dimension_semantics=sem, vmem_limit_bytes=vmem)


def _rms(x, g):
    return x * lax.rsqrt(jnp.mean(x * x, axis=-1, keepdims=True) + RMS_EPS) * g


def _mod_body(c_ref, w_ref, b_ref, o_ref):
    c = c_ref[...]
    a = c * jax.nn.sigmoid(c)
    o_ref[0] = jnp.dot(a, w_ref[0], preferred_element_type=F32,
                       precision=lax.Precision.HIGHEST) + b_ref[0]


def _modulation(cond8, w_mod, b_mod):
    tn = 1536
    n = 6 * D_MODEL
    return pl.pallas_call(
        _mod_body,
        grid=(DEPTH, n // tn),
        in_specs=[pl.BlockSpec((8, D_MODEL), lambda l, j: (0, 0)),
                  pl.BlockSpec((1, D_MODEL, tn), lambda l, j: (l, 0, j)),
                  pl.BlockSpec((1, 1, tn), lambda l, j: (l, 0, j))],
        out_specs=pl.BlockSpec((1, 8, tn), lambda l, j: (l, 0, j)),
        out_shape=jax.ShapeDtypeStruct((DEPTH, 8, n), F32),
        compiler_params=_cparams(("arbitrary", "arbitrary")),
        name="modulation",
    )(cond8, w_mod, b_mod.reshape(DEPTH, 1, n))


def _rope_tables():
    t = np.arange(SEG)
    row = (t // GRID_W).astype(np.float64)
    col = (t % GRID_W).astype(np.float64)
    quarter = HEAD_DIM // 4
    inv = ROPE_BASE ** (-np.arange(quarter, dtype=np.float64) / quarter)
    lane = np.arange(128)
    d = lane % HEAD_DIM
    pos = np.where((d // 32)[None, :] == 0, row[:, None], col[:, None])
    ang = pos * inv[d % quarter][None, :]
    first = ((d % 32) < quarter)[None, :]
    cos = np.cos(ang)
    sin = np.sin(ang)
    sin_a = np.where(first, -sin, 0.0)
    sin_b = np.where(first, 0.0, sin)
    ident = np.zeros((SEG, 128))
    tab = lambda ctx, lat: jnp.asarray(np.concatenate([ctx, lat], axis=0), F32)
    return tab(ident + 1.0, cos), tab(ident, sin_a), tab(ident, sin_b)


def _rope(x, cos, sin_a, sin_b):
    w = x.shape[-1]
    rep = w // 128
    if rep > 1:
        cos = jnp.concatenate([cos] * rep, axis=1)
        sin_a = jnp.concatenate([sin_a] * rep, axis=1)
        sin_b = jnp.concatenate([sin_b] * rep, axis=1)
    quarter = HEAD_DIM // 4
    return x * cos + pltpu.roll(x, w - quarter, 1) * sin_a + pltpu.roll(x, quarter, 1) * sin_b


TM_TOK = SEQ
N_CTX_TILES = T_CTX // TM_TOK
LAT_TILES = DEC_SEQ // TM_TOK
N_TILES = T_ALL // TM_TOK


def _ctx_tm_spec(width):
    return pl.BlockSpec((TM_TOK, width), lambda i: (0, jnp.minimum(i, N_CTX_TILES - 1)))


def _lat_tm_spec(width):
    def idx(i):
        j = jnp.maximum(i - N_CTX_TILES, 0)
        return (j % LAT_TILES, j // LAT_TILES)
    return pl.BlockSpec((TM_TOK, width), idx)


def _proj_body(x_ref, g_ref, sh_ref, sc_ref, w_ref, cos_ref, sa_ref, sb_ref,
               q_ref, k_ref, v_ref, hyc_ref, sc_out_ref, hyl_ref, sl_out_ref):
    i = pl.program_id(0)
    h = _rms(x_ref[...], g_ref[...]) * (1.0 + sc_ref[0]) + sh_ref[0]
    p = jnp.dot(h.astype(BF16), w_ref[...], preferred_element_type=F32)
    cos, sa, sb = cos_ref[...], sa_ref[...], sb_ref[...]
    q_ref[...] = _rope(p[:, O_HY:O_Q], cos, sa, sb)
    k_ref[...] = _rope(p[:, O_Q:O_K], cos, sa, sb)
    v_ref[...] = p[:, O_K:O_V]

    @pl.when(i < N_CTX_TILES)
    def _():
        hyc_ref[...] = p[:, :O_HY]
        sc_out_ref[...] = p[:, O_V:O_S]

    @pl.when(i >= N_CTX_TILES)
    def _():
        hyl_ref[...] = p[:, :O_HY]
        sl_out_ref[...] = p[:, O_V:O_S]


def _seg_spec(tm):
    return pl.BlockSpec((1, 1, D_MODEL), lambda i: ((i * tm) // SEG, 0, 0))


def _const_spec(shape):
    nd = len(shape)
    return pl.BlockSpec(shape, lambda i: (0,) * nd, pipeline_mode=pl.Buffered(1))


def _in_proj(x, gain, sh, sc, w_mix, rope_tabs):
    tm = TM_TOK
    per_seg = SEG // tm
    rope_spec = pl.BlockSpec((tm, 128), lambda i: (jnp.where(i < per_seg, i, per_seg + i % per_seg), 0))
    tok = lambda w: pl.BlockSpec((tm, w), lambda i: (i, 0))
    tok_shape = lambda w: jax.ShapeDtypeStruct((T_ALL, w), F32)
    return pl.pallas_call(
        _proj_body,
        grid=(N_TILES,),
        in_specs=[tok(D_MODEL), _const_spec((1, D_MODEL)), _seg_spec(tm), _seg_spec(tm),
                  _const_spec((D_MODEL, MIX_WIDTH)), rope_spec, rope_spec, rope_spec],
        out_specs=[tok(ATT_WIDTH), tok(KV_WIDTH), tok(KV_WIDTH),
                   _ctx_tm_spec(O_HY), _ctx_tm_spec(S5_WIDTH), _lat_tm_spec(O_HY), _lat_tm_spec(S5_WIDTH)],
        out_shape=[tok_shape(ATT_WIDTH), tok_shape(KV_WIDTH), tok_shape(KV_WIDTH),
                   jax.ShapeDtypeStruct((SEQ, BATCH * O_HY), F32),
                   jax.ShapeDtypeStruct((SEQ, BATCH * S5_WIDTH), F32),
                   jax.ShapeDtypeStruct((DEC_SEQ, DEC_BATCH * O_HY), F32),
                   jax.ShapeDtypeStruct((DEC_SEQ, DEC_BATCH * S5_WIDTH), F32)],
        compiler_params=_cparams(("arbitrary",)),
        name="in_proj",
    )(x, gain, sh, sc, w_mix, *rope_tabs)


def _attend(q, keys, vals, masks, sink_ref):
    qb = q.shape[0]
    scale = HEAD_DIM ** -0.5
    grp = lax.broadcasted_iota(jnp.int32, (GQ * qb, 1), 0) // qb
    outs = []
    for h in range(N_KV_HEADS):
        lo = h * HEAD_DIM
        qs = jnp.concatenate([q[:, (h * GQ + g) * HEAD_DIM:(h * GQ + g + 1) * HEAD_DIM]
                              for g in range(GQ)], axis=0)
        qs = (qs * scale).astype(BF16)
        sink = jnp.zeros((GQ * qb, 1), F32)
        for g in range(GQ):
            sink = jnp.where(grp == g, sink_ref[h * GQ + g], sink)
        logits = []
        for k_i, m_i in zip(keys, masks):
            s = lax.dot_general(qs, k_i[:, lo:lo + HEAD_DIM].astype(BF16),
                                (((1,), (1,)), ((), ())), preferred_element_type=F32)
            if m_i is not None:
                s = jnp.where(jnp.concatenate([m_i] * GQ, axis=0), s, NEG_INF)
            logits.append(s)
        m = sink
        for s in logits:
            m = jnp.maximum(m, jnp.max(s, axis=-1, keepdims=True))
        denom = jnp.exp(sink - m)
        acc = jnp.zeros((GQ * qb, HEAD_DIM), F32)
        for s, v_i in zip(logits, vals):
            p = jnp.exp(s - m)
            denom = denom + jnp.sum(p, axis=-1, keepdims=True)
            acc = acc + jnp.dot(p.astype(BF16), v_i[:, lo:lo + HEAD_DIM].astype(BF16),
                                preferred_element_type=F32)
        o = acc / denom
        outs.extend(o[g * qb:(g + 1) * qb] for g in range(GQ))
    return jnp.concatenate(outs, axis=1)


N_QB_CTX = T_CTX // Q_BLOCK
QB_PER_LAT = DEC_SEQ // Q_BLOCK


def _attn_body(sink_ref, q_ref, kp_ref, kc_ref, kn_ref, vp_ref, vc_ref, vn_ref, ck_ref, cv_ref, o_ref):
    g = pl.program_id(0)

    @pl.when(g < N_QB_CTX)
    def _():
        o_ref[0] = _attend(q_ref[0], [ck_ref[0]], [cv_ref[0]], [None], sink_ref)

    @pl.when(g >= N_QB_CTX)
    def _():
        i = (g - N_QB_CTX) % QB_PER_LAT
        kw = jnp.concatenate([kp_ref[0], kc_ref[0], kn_ref[0]], axis=0)
        vw = jnp.concatenate([vp_ref[0], vc_ref[0], vn_ref[0]], axis=0)
        qpos = i * Q_BLOCK + lax.broadcasted_iota(jnp.int32, (Q_BLOCK, 3 * Q_BLOCK), 0)
        kpos = (i - 1) * Q_BLOCK + lax.broadcasted_iota(jnp.int32, (Q_BLOCK, 3 * Q_BLOCK), 1)
        band = (jnp.abs(qpos - kpos) <= WINDOW) & (kpos >= 0) & (kpos < DEC_SEQ)
        o_ref[0] = _attend(q_ref[0], [kw, ck_ref[0]], [vw, cv_ref[0]], [band, None], sink_ref)


def _attention(q, k, v, ck, cv, sink):
    smem = pl.BlockSpec(memory_space=pltpu.SMEM)
    nqb = T_ALL // Q_BLOCK
    q3 = q.reshape(nqb, Q_BLOCK, ATT_WIDTH)
    k3 = k.reshape(nqb, Q_BLOCK, KV_WIDTH)
    v3 = v.reshape(nqb, Q_BLOCK, KV_WIDTH)
    ck_all = jnp.concatenate([k[:T_CTX].reshape(BATCH, SEQ, KV_WIDTH), ck], axis=0)
    cv_all = jnp.concatenate([v[:T_CTX].reshape(BATCH, SEQ, KV_WIDTH), cv], axis=0)

    def win(off):
        def idx(g):
            gl = jnp.maximum(g - N_QB_CTX, 0)
            base = N_QB_CTX + (gl // QB_PER_LAT) * QB_PER_LAT
            j = jnp.clip(gl % QB_PER_LAT + off, 0, QB_PER_LAT - 1)
            return (jnp.where(g < N_QB_CTX, g, base + j), 0, 0)
        return pl.BlockSpec((1, Q_BLOCK, KV_WIDTH), idx)

    def cache_idx(g):
        return (jnp.where(g < N_QB_CTX, g // (SEQ // Q_BLOCK), BATCH + (g - N_QB_CTX) // QB_PER_LAT), 0, 0)

    cache_spec = pl.BlockSpec((1, PAST_LEN, KV_WIDTH), cache_idx)
    qo_spec = pl.BlockSpec((1, Q_BLOCK, ATT_WIDTH), lambda g: (g, 0, 0))
    att = pl.pallas_call(
        _attn_body,
        grid=(nqb,),
        in_specs=[smem, qo_spec, win(-1), win(0), win(1), win(-1), win(0), win(1), cache_spec, cache_spec],
        out_specs=qo_spec,
        out_shape=jax.ShapeDtypeStruct((nqb, Q_BLOCK, ATT_WIDTH), F32),
        compiler_params=_cparams(("arbitrary",)),
        name="attention",
    )(sink, q3, k3, k3, k3, v3, v3, v3, ck_all, cv_all)
    return att.reshape(T_ALL, ATT_WIDTH)


def _hyena_filters(L, p):
    t = jnp.arange(L, dtype=F32)
    t_norm = t / max(L - 1, 1)
    bands = jnp.arange(1, HY_BANDS + 1, dtype=F32)
    ang = (2.0 * math.pi / L) * t[:, None] * bands[None, :]
    z = jnp.concatenate([t_norm[:, None], jnp.cos(ang), jnp.sin(ang)], axis=-1)
    h = jnp.sin(p["hy_f_freq"][0] * (z @ p["hy_f_w1"] + p["hy_f_b1"]))
    h = jnp.sin(p["hy_f_freq"][1] * (h @ p["hy_f_w2"] + p["hy_f_b2"]))
    h = (h @ p["hy_f_w3"]).astype(F32).reshape(L, HY_ORDER, 2, HY_WIDTH)
    h = h * jnp.exp(-t_norm[:, None, None, None] * jnp.abs(p["hy_decay"].astype(F32))[None])
    return h / (jnp.sum(jnp.abs(h), axis=(0, 2), keepdims=True) + 1e-6)


def _short_conv_body(x_ref, w_ref, b_ref, o_ref):
    x = x_ref[...]
    L = x.shape[0]
    t = lax.broadcasted_iota(jnp.int32, x.shape, 0)
    prev = jnp.where(t == 0, 0.0, pltpu.roll(x, 1, 0))
    nxt = jnp.where(t == L - 1, 0.0, pltpu.roll(x, L - 1, 0))
    o_ref[0] = prev * w_ref[0:1, :] + x * w_ref[1:2, :] + nxt * w_ref[2:3, :] + b_ref[...]


def _short_conv_tm(x, nb, w, b):
    L = x.shape[0]
    nct = O_HY // 128
    per = HY_WIDTH // 128
    return pl.pallas_call(
        _short_conv_body,
        grid=(nb, nct),
        in_specs=[pl.BlockSpec((L, 128), lambda bi, j: (0, bi * nct + j)),
                  pl.BlockSpec((3, 128), lambda bi, j: (0, j)),
                  pl.BlockSpec((1, 128), lambda bi, j: (0, j))],
        out_specs=pl.BlockSpec((1, L, 128), lambda bi, j: (j // per, 0, bi * per + j % per)),
        out_shape=jax.ShapeDtypeStruct((3, L, nb * HY_WIDTH), F32),
        compiler_params=_cparams(("arbitrary", "arbitrary")),
        name="hyena_short_conv",
    )(x, w, b[None])


class _FftPlan:
    def __init__(self, L, n1, n2):
        N = 2 * L
        assert n1 * n2 == N
        self.L, self.N, self.n1, self.n2 = L, N, n1, n2
        self.kin = n1 // 2
        k1 = np.arange(n1)
        th = 2.0 * np.pi * np.outer(k1, np.arange(n1)) / n1
        f_re, f_im = np.cos(th), -np.sin(th)
        self._mats = {"f_full": np.concatenate([f_re, f_im], axis=0),
                      "f_half": np.concatenate([f_re, f_im], axis=0)[:, :self.kin],
                      "f_inv": np.concatenate([f_re.T, f_im.T], axis=1)[:self.kin]}
        if n2 > 1:
            k2 = np.arange(n2)
            idx = (np.outer(k2, k2)[None] * n1 + k1[:, None, None] * k2[None, None, :]) % N
            ph = 2.0 * np.pi * idx / N
            g_re, g_im = np.cos(ph), -np.sin(ph)
            blk = lambda a, b: np.concatenate([np.concatenate([a, -b], axis=2),
                                               np.concatenate([b, a], axis=2)], axis=1)
            self._mats["g"] = blk(g_re, g_im)
            h_re, h_im = np.swapaxes(g_re, 1, 2), -np.swapaxes(g_im, 1, 2)
            self._mats["gh"] = blk(h_re, h_im)

    def __getattr__(self, name):
        mats = self.__dict__.get("_mats", {})
        if name in mats:
            return jnp.asarray(mats[name].astype(np.float32)).astype(BF16)
        raise AttributeError(name)


def _fft_s1_body(x_ref, f_ref, o_ref):
    o_ref[...] = jnp.dot(f_ref[...], x_ref[0].astype(BF16), preferred_element_type=F32).astype(o_ref.dtype)


def _fft_s1(x3, plane, f, out_dtype=BF16):
    _, kin, m = x3.shape
    rows = f.shape[0]
    tn = min(m, (4 * 1024 * 1024) // (rows * 2))
    return pl.pallas_call(
        _fft_s1_body,
        grid=(m // tn,),
        in_specs=[pl.BlockSpec((1, kin, tn), lambda j: (plane, 0, j)), pl.BlockSpec(f.shape, lambda j: (0, 0))],
        out_specs=pl.BlockSpec((rows, tn), lambda j: (0, j)),
        out_shape=jax.ShapeDtypeStruct((rows, m), out_dtype),
        compiler_params=_cparams(("arbitrary",)),
        name="fft_stage1",
    )(x3, f)


def _cmul_tiled(x_re, x_im, k_re, k_im):
    rep = x_re.shape[-1] // k_re.shape[-1]
    if rep > 1:
        k_re = jnp.concatenate([k_re] * rep, axis=-1)
        k_im = jnp.concatenate([k_im] * rep, axis=-1)
    return x_re * k_re - x_im * k_im, x_re * k_im + x_im * k_re


def _fft_mid_body(a_ref, g_ref, gh_ref, kf_ref, o_ref, *, kb, n2):
    for j in range(kb):
        a = jnp.concatenate([a_ref[0, j], a_ref[1, j]], axis=0)
        x = jnp.dot(g_ref[j], a, preferred_element_type=F32)
        y_re, y_im = _cmul_tiled(x[:n2], x[n2:], kf_ref[0, j], kf_ref[1, j])
        y = jnp.concatenate([y_re, y_im], axis=0).astype(BF16)
        b = jnp.dot(gh_ref[j], y, preferred_element_type=F32)
        o_ref[0, j] = b[:n2].astype(o_ref.dtype)
        o_ref[1, j] = b[n2:].astype(o_ref.dtype)


def _fft_fwd2_body(a_ref, g_ref, o_ref, *, kb, n2):
    for j in range(kb):
        a = jnp.concatenate([a_ref[0, j], a_ref[1, j]], axis=0)
        x = jnp.dot(g_ref[j], a, preferred_element_type=F32)
        o_ref[0, j] = x[:n2]
        o_ref[1, j] = x[n2:]


def _fft_mid(plan, a, kf=None, order=0):
    n1, n2 = plan.n1, plan.n2
    w = a.shape[1] // n2
    a4 = a.reshape(2, n1, n2, w)
    kb = 8
    a_spec = pl.BlockSpec((2, kb, n2, w), lambda i: (0, i, 0, 0))
    g_spec = pl.BlockSpec((kb, 2 * n2, 2 * n2), lambda i: (i, 0, 0))
    if kf is None:
        return pl.pallas_call(
            functools.partial(_fft_fwd2_body, kb=kb, n2=n2),
            grid=(n1 // kb,),
            in_specs=[a_spec, g_spec],
            out_specs=a_spec,
            out_shape=jax.ShapeDtypeStruct((2, n1, n2, w), F32),
            compiler_params=_cparams(("arbitrary",)),
            name="fft_stage2",
        )(a4, plan.g)
    out = pl.pallas_call(
        functools.partial(_fft_mid_body, kb=kb, n2=n2),
        grid=(n1 // kb,),
        in_specs=[a_spec, g_spec, g_spec,
                  pl.BlockSpec((2, kb, n2, HY_WIDTH), lambda i: (0, i, 0, order))],
        out_specs=a_spec,
        out_shape=jax.ShapeDtypeStruct((2, n1, n2, w), BF16),
        compiler_params=_cparams(("arbitrary",)),
        name="fft_mid",
    )(a4, plan.g, plan.gh, kf)
    return out.reshape(2 * n1, n2 * w)


def _fft_inv_body(*refs, n1, inv_n, spectral, chain):
    b_ref, fi_ref, u_ref, gate_ref, bias_ref = refs[:5]
    rest = list(refs[5:])
    bm = b_ref[...]
    if spectral:
        kf_ref = rest.pop(0)
        a = bm.astype(F32)
        y_re, y_im = _cmul_tiled(a[:n1], a[n1:], kf_ref[:n1, :], kf_ref[n1:, :])
        bm = jnp.concatenate([y_re, y_im], axis=0).astype(BF16)
    y = jnp.dot(fi_ref[...], bm, preferred_element_type=F32) * inv_n
    u = u_ref[0]
    z = gate_ref[0] * (y + u * bias_ref[...])
    if chain:
        fs_ref, z_ref, a2_ref = rest
        z_ref[0] = z
        a2_ref[...] = jnp.dot(fs_ref[...], z.astype(BF16), preferred_element_type=F32).astype(BF16)
    else:
        (z_ref,) = rest
        z_ref[0] = z


def _fft_inv(plan, bm, u3, u_plane, gate3, gate_plane, bias_t, kf=None, order=0, chain=False):
    kin, n1 = plan.kin, plan.n1
    m = bm.shape[1]
    tn = bias_t.shape[1]
    in_specs = [pl.BlockSpec((2 * n1, tn), lambda j: (0, j)),
                pl.BlockSpec((kin, 2 * n1), lambda j: (0, 0)),
                pl.BlockSpec((1, kin, tn), lambda j: (u_plane, 0, j)),
                pl.BlockSpec((1, kin, tn), lambda j: (gate_plane, 0, j)),
                pl.BlockSpec((1, tn), lambda j: (0, 0))]
    args = [bm, plan.f_inv, u3, gate3, bias_t]
    if kf is not None:
        in_specs.append(pl.BlockSpec((2 * n1, HY_WIDTH), lambda j: (0, order)))
        args.append(kf)
    out_specs = [pl.BlockSpec((1, kin, tn), lambda j: (0, 0, j))]
    out_shape = [jax.ShapeDtypeStruct((1, kin, m), F32)]
    if chain:
        in_specs.append(pl.BlockSpec((2 * n1, kin), lambda j: (0, 0)))
        args.append(plan.f_half)
        out_specs.append(pl.BlockSpec((2 * n1, tn), lambda j: (0, j)))
        out_shape.append(jax.ShapeDtypeStruct((2 * n1, m), BF16))
    return pl.pallas_call(
        functools.partial(_fft_inv_body, n1=n1, inv_n=1.0 / plan.N, spectral=kf is not None, chain=chain),
        grid=(m // tn,),
        in_specs=in_specs,
        out_specs=out_specs,
        out_shape=out_shape,
        compiler_params=_cparams(("arbitrary",)),
        name="fft_inverse",
    )(*args)


_PLAN_CTX = _FftPlan(SEQ, 2 * SEQ, 1)
_PLAN_LAT = _FftPlan(DEC_SEQ, 128, 64)


def _hyena_group(x, nb, p, plan):
    L = x.shape[0]
    w = nb * HY_WIDTH
    kin, n2 = plan.kin, plan.n2
    m = n2 * w
    sc = _short_conv_tm(x, nb, p["hy_short_w"], p["hy_short_b"]).reshape(3, kin, m)
    h = _hyena_filters(L, p)
    kk = jnp.concatenate([h[:, :, 0], jnp.zeros((1, HY_ORDER, HY_WIDTH), F32), h[:0:-1, :, 1]], axis=0)
    kk = kk.reshape(1, plan.n1, n2 * HY_ORDER * HY_WIDTH)
    kf = _fft_s1(kk, 0, plan.f_full, out_dtype=BF16 if n2 > 1 else F32)
    tn = min(m, 2048 if n2 == 1 else 8192)
    bias_t = [jnp.tile(p["hy_bias"][o], tn // HY_WIDTH)[None] for o in range(HY_ORDER)]
    a = _fft_s1(sc, 2, plan.f_half)
    if n2 > 1:
        kf = _fft_mid(plan, kf)
        bm = _fft_mid(plan, a, kf, 0)
        z, a2 = _fft_inv(plan, bm, sc, 2, sc, 0, bias_t[0], chain=True)
        bm2 = _fft_mid(plan, a2, kf, 1)
        (hy,) = _fft_inv(plan, bm2, z, 0, sc, 1, bias_t[1])
    else:
        z, a2 = _fft_inv(plan, a, sc, 2, sc, 0, bias_t[0], kf=kf, order=0, chain=True)
        (hy,) = _fft_inv(plan, a2, z, 0, sc, 1, bias_t[1], kf=kf, order=1)
    return hy.reshape(L, w)


S5_LANES = S5_GROUPS * S5_STATE
S5_CHUNK_ROWS = 1024


def _s5_spread(nb, tc):
    rows = 2 * nb
    e_f = np.zeros((tc * rows, tc * nb), np.float32)
    e_b = np.zeros((tc * rows, tc * nb), np.float32)
    t, b = np.meshgrid(np.arange(tc), np.arange(nb), indexing="ij")
    e_f[t * rows + b, t * nb + b] = 1.0
    e_b[t * rows + nb + b, (tc - 1 - t) * nb + b] = 1.0
    return e_f, e_b


def _s5_body(uf_ref, ub_ref, h0_ref, are_ref, aim_ref, ef_ref, eb_ref, eft_ref, ebt_ref, b_ref, c_ref,
             yf_ref, yb_ref, hfin_ref, s_ref, *, rows, tc):
    i = pl.program_id(0)
    dot = functools.partial(jnp.dot, preferred_element_type=F32)

    @pl.when(i == 0)
    def _():
        s_ref[pl.ds(0, rows), :] = h0_ref[...]

    u2 = jnp.concatenate([dot(ef_ref[...], uf_ref[...].astype(BF16)),
                          dot(eb_ref[...], ub_ref[...].astype(BF16))], axis=1).astype(BF16)
    s_ref[pl.ds(rows, tc * rows), :] = dot(u2, b_ref[...])
    re = pl.ds(0, S5_LANES)
    im = pl.ds(S5_LANES, S5_LANES)

    def step(t, carry):
        for r in range(rows // 8):
            src = pl.multiple_of(t * rows + r * 8, 8)
            dst = pl.multiple_of(src + rows, 8)
            a_re = are_ref[pl.ds(r * 8, 8), :]
            a_im = aim_ref[pl.ds(r * 8, 8), :]
            c_re = s_ref[pl.ds(src, 8), re]
            c_im = s_ref[pl.ds(src, 8), im]
            s_ref[pl.ds(dst, 8), re] = a_re * c_re - a_im * c_im + s_ref[pl.ds(dst, 8), re]
            s_ref[pl.ds(dst, 8), im] = a_re * c_im + a_im * c_re + s_ref[pl.ds(dst, 8), im]
        return carry

    lax.fori_loop(0, tc, step, 0)
    last = s_ref[pl.ds(tc * rows, rows), :]
    hfin_ref[...] = last
    y = dot(s_ref[pl.ds(rows, tc * rows), :].astype(BF16), c_ref[...])
    hi = y.astype(BF16)
    lo = (y - hi.astype(F32)).astype(BF16)
    yf_ref[...] = dot(eft_ref[...], hi[:, :S5_WIDTH]) + dot(eft_ref[...], lo[:, :S5_WIDTH])
    yb_ref[...] = dot(ebt_ref[...], hi[:, S5_WIDTH:]) + dot(ebt_ref[...], lo[:, S5_WIDTH:])
    s_ref[pl.ds(0, rows), :] = last


def _s5_params(p):
    a_re = jnp.minimum(p["s5_a_re"], -1e-4)
    a_im = p["s5_a_im"]
    dt = jnp.exp(p["s5_log_dt"])[..., None]
    mag = jnp.exp(a_re * dt)
    ab_re, ab_im = mag * jnp.cos(a_im * dt), mag * jnp.sin(a_im * dt)
    den = a_re * a_re + a_im * a_im
    co_re = ((ab_re - 1.0) * a_re + ab_im * a_im) / den
    co_im = (ab_im * a_re - (ab_re - 1.0) * a_im) / den
    bb_re = co_re[..., None] * p["s5_b_re"] - co_im[..., None] * p["s5_b_im"]
    bb_im = co_re[..., None] * p["s5_b_im"] + co_im[..., None] * p["s5_b_re"]
    eye = jnp.eye(S5_GROUPS, dtype=F32)
    blk_in = lambda m: jnp.einsum("dgnp,gh->dgphn", m, eye).reshape(2 * S5_WIDTH, S5_LANES)
    b_cat = jnp.concatenate([blk_in(bb_re), blk_in(bb_im)], axis=1)
    blk_out = lambda m: jnp.einsum("dgpn,gh->hndgp", m, eye).reshape(S5_LANES, 2 * S5_WIDTH)
    c_cat = jnp.concatenate([blk_out(p["s5_c_re"]), blk_out(-p["s5_c_im"])], axis=0)
    return (ab_re.reshape(2, S5_LANES), ab_im.reshape(2, S5_LANES), b_cat.astype(BF16), c_cat.astype(BF16))


def _s5_group(u_tm, nb, h0, prm):
    ab_re, ab_im, b_cat, c_cat = prm
    L = u_tm.shape[0]
    rows = 2 * nb
    tc = S5_CHUNK_ROWS // rows
    nc = L // tc
    e_f, e_b = _s5_spread(nb, tc)
    bf = lambda m: jnp.asarray(m, BF16)
    are = jnp.repeat(ab_re, nb, axis=0)
    aim = jnp.repeat(ab_im, nb, axis=0)
    const = lambda shape: pl.BlockSpec(shape, lambda i: (0, 0))
    fwd = pl.BlockSpec((tc * nb, S5_WIDTH), lambda i: (i, 0))
    bwd = pl.BlockSpec((tc * nb, S5_WIDTH), lambda i: (nc - 1 - i, 0))
    u2 = u_tm.reshape(L * nb, S5_WIDTH)
    spread, gather = (tc * rows, tc * nb), (tc * nb, tc * rows)
    y_f, y_b, hfin = pl.pallas_call(
        functools.partial(_s5_body, rows=rows, tc=tc),
        grid=(nc,),
        in_specs=[fwd, bwd, const((rows, 2 * S5_LANES)), const((rows, S5_LANES)), const((rows, S5_LANES)),
                  const(spread), const(spread), const(gather), const(gather),
                  const((2 * S5_WIDTH, 2 * S5_LANES)), const((2 * S5_LANES, 2 * S5_WIDTH))],
        out_specs=[fwd, bwd, const((rows, 2 * S5_LANES))],
        out_shape=[jax.ShapeDtypeStruct((L * nb, S5_WIDTH), F32), jax.ShapeDtypeStruct((L * nb, S5_WIDTH), F32),
                   jax.ShapeDtypeStruct((rows, 2 * S5_LANES), F32)],
        scratch_shapes=[pltpu.VMEM(((tc + 1) * rows, 2 * S5_LANES), F32)],
        compiler_params=_cparams(("arbitrary",)),
        name="s5_scan",
    )(u2, u2, h0, are, aim, bf(e_f), bf(e_b), bf(e_f.T), bf(e_b.T), b_cat, c_cat)
    return y_f.reshape(L, nb * S5_WIDTH), y_b.reshape(L, nb * S5_WIDTH), hfin


def _s5_scan(s_ctx, s_lat, p, st_lat):
    prm = _s5_params(p)
    h0_ctx = jnp.zeros((2 * BATCH, 2 * S5_LANES), F32)
    h0_lat = jnp.transpose(st_lat, (1, 0, 4, 2, 3)).reshape(2 * DEC_BATCH, 2 * S5_LANES)
    yf_c, yb_c, hfin = _s5_group(s_ctx, BATCH, h0_ctx, prm)
    yf_l, yb_l, _ = _s5_group(s_lat, DEC_BATCH, h0_lat, prm)
    new_state = jnp.transpose(hfin.reshape(2, BATCH, 2, S5_GROUPS, S5_STATE), (1, 0, 3, 4, 2))
    return (yf_c, yb_c, yf_l, yb_l), new_state


def _merge_body(x_ref, att_ref, hyc_ref, hyl_ref, yfc_ref, yfl_ref, ybc_ref, ybl_ref, sc_in_ref, sl_in_ref,
                sh1_ref, sc1_ref, g1_ref, sh2_ref, sc2_ref,
                np0_ref, npost0_ref, np1_ref, d_ref, gluw_ref, glub_ref,
                wg_ref, wh_ref, wa_ref, ws_ref, wo_ref, rw_ref, rb_ref,
                x1_ref, h2_ref, ti_ref, tg_ref, rank_ref, cnt_ref, carry_ref):
    i = pl.program_id(0)
    is_ctx = i < N_CTX_TILES
    pick = lambda c_ref, l_ref: jnp.where(is_ctx, c_ref[...], l_ref[...])
    x = x_ref[...]
    h = (_rms(x, np0_ref[...]) * (1.0 + sc1_ref[0]) + sh1_ref[0]).astype(BF16)
    dot = functools.partial(jnp.dot, preferred_element_type=F32)
    ys = pick(yfc_ref, yfl_ref) + pick(ybc_ref, ybl_ref) + pick(sc_in_ref, sl_in_ref) * d_ref[...]
    ys = jax.nn.gelu(ys)
    g = dot(ys.astype(BF16), gluw_ref[...]) + glub_ref[...]
    s = g[:, :S5_WIDTH] * jax.nn.sigmoid(g[:, S5_WIDTH:])
    hy = pick(hyc_ref, hyl_ref)
    merged = jax.nn.sigmoid(dot(h, wg_ref[:, :D_MODEL])) * dot(hy.astype(BF16), wh_ref[...])
    merged += jax.nn.sigmoid(dot(h, wg_ref[:, D_MODEL:2 * D_MODEL])) * dot(att_ref[...].astype(BF16), wa_ref[...])
    merged += jax.nn.sigmoid(dot(h, wg_ref[:, 2 * D_MODEL:])) * dot(s.astype(BF16), ws_ref[...])
    m = dot(merged.astype(BF16), wo_ref[...])
    x1 = x + g1_ref[0] * _rms(m, npost0_ref[...])
    x1_ref[...] = x1
    h2 = _rms(x1, np1_ref[...]) * (1.0 + sc2_ref[0]) + sh2_ref[0]
    h2_ref[...] = h2
    logits = jnp.dot(h2, rw_ref[...], preferred_element_type=F32,
                     precision=lax.Precision.HIGHEST) + rb_ref[...]
    lane = lax.broadcasted_iota(jnp.int32, logits.shape, 1)
    vals, idxs = [], []
    for _ in range(TOP_K):
        mx = jnp.max(logits, axis=-1, keepdims=True)
        ix = jnp.min(jnp.where(logits == mx, lane, N_EXPERTS), axis=-1, keepdims=True)
        vals.append(mx)
        idxs.append(ix)
        logits = jnp.where(lane == ix, -jnp.inf, logits)
    e = [jnp.exp(v - vals[0]) for v in vals]
    tot = e[0] + e[1] + e[2] + e[3]
    ti_ref[...] = jnp.concatenate(idxs, axis=1)
    tg_ref[...] = jnp.concatenate(e, axis=1) / tot

    @pl.when(i == 0)
    def _():
        carry_ref[...] = jnp.zeros_like(carry_ref)

    tm = x.shape[0]
    onehots = [(lane == ix).astype(F32) for ix in idxs]
    chosen = onehots[0] + onehots[1] + onehots[2] + onehots[3]
    earlier = (lax.broadcasted_iota(jnp.int32, (tm, tm), 1)
               < lax.broadcasted_iota(jnp.int32, (tm, tm), 0)).astype(BF16)
    base = carry_ref[...] + dot(earlier, chosen.astype(BF16))
    rank_ref[...] = jnp.concatenate([jnp.sum(oh * base, axis=-1, keepdims=True) for oh in onehots],
                                    axis=1).astype(jnp.int32)
    carry_ref[...] = carry_ref[...] + jnp.sum(chosen, axis=0, keepdims=True)
    cnt_ref[...] = carry_ref[...]


def _merge(x, att, seq_c, seq_l, mods, vecs, mats):
    tm = TM_TOK
    tok = lambda w: pl.BlockSpec((tm, w), lambda i: (i, 0))
    seq_specs, seq_args = [], []
    for c, l in zip(seq_c, seq_l):
        seq_specs += [_ctx_tm_spec(HY_WIDTH), _lat_tm_spec(HY_WIDTH)]
        seq_args += [c, l]
    in_specs = ([tok(D_MODEL), tok(ATT_WIDTH)] + seq_specs
                + [_seg_spec(tm)] * len(mods)
                + [_const_spec(v.shape) for v in vecs]
                + [_const_spec(m.shape) for m in mats])
    return pl.pallas_call(
        _merge_body,
        grid=(N_TILES,),
        in_specs=in_specs,
        out_specs=[tok(D_MODEL), tok(D_MODEL), tok(TOP_K), tok(TOP_K), tok(TOP_K),
                   pl.BlockSpec((1, N_EXPERTS), lambda i: (0, 0))],
        out_shape=[jax.ShapeDtypeStruct((T_ALL, D_MODEL), F32),
                   jax.ShapeDtypeStruct((T_ALL, D_MODEL), F32),
                   jax.ShapeDtypeStruct((T_ALL, TOP_K), jnp.int32),
                   jax.ShapeDtypeStruct((T_ALL, TOP_K), F32),
                   jax.ShapeDtypeStruct((T_ALL, TOP_K), jnp.int32),
                   jax.ShapeDtypeStruct((1, N_EXPERTS), F32)],
        scratch_shapes=[pltpu.VMEM((1, N_EXPERTS), F32)],
        compiler_params=_cparams(("arbitrary",)),
        name="merge_router",
    )(x, att, *seq_args, *mods, *vecs, *mats)


def _gather_rows_start(src_hbm, idx_ref, buf, sem):
    def body(j, carry):
        pltpu.make_async_copy(src_hbm.at[pl.ds(idx_ref[0, 0, j], 1)], buf.at[pl.ds(j, 1)], sem).start()
        return carry

    lax.fori_loop(0, buf.shape[0], body, 0, unroll=8)


def _gather_rows_wait(src_hbm, buf, sem):
    pltpu.make_async_copy(src_hbm.at[pl.ds(0, buf.shape[0])], buf, sem).wait()


def _double_buffered_gather(i, n_active, src_hbm, idx_cur, idx_nxt, bufs, sems, compute):
    for par in range(2):
        @pl.when((i % 2 == par) & (i < n_active))
        def _(par=par):
            if par == 0:
                @pl.when(i == 0)
                def _():
                    _gather_rows_start(src_hbm, idx_cur, bufs[0], sems.at[0])

            @pl.when(i + 1 < n_active)
            def _():
                _gather_rows_start(src_hbm, idx_nxt, bufs[1 - par], sems.at[1 - par])

            _gather_rows_wait(src_hbm, bufs[par], sems.at[par])
            compute(bufs[par])


def _moe_body(be_ref, first_ref, nused_ref, idx_cur, idx_nxt, h2_hbm, w1_ref, b1_ref, w2_ref, b2_ref,
              o_ref, w1b_ref, w2b_ref, xs0, xs1, sems):
    i = pl.program_id(0)

    @pl.when(first_ref[i] == 1)
    def _():
        w1b_ref[...] = w1_ref[0].astype(BF16)
        w2b_ref[...] = w2_ref[0].astype(BF16)

    def compute(xs_ref):
        h = jnp.dot(xs_ref[...].astype(BF16), w1b_ref[...], preferred_element_type=F32) + b1_ref[0]
        glu = jnp.minimum(h[:, :D_EXPERT], SWIGLU_LIMIT)
        lin = jnp.clip(h[:, D_EXPERT:], -SWIGLU_LIMIT, SWIGLU_LIMIT)
        o = glu * jax.nn.sigmoid(SWIGLU_ALPHA * glu) * (lin + 1.0)
        o_ref[...] = jnp.dot(o.astype(BF16), w2b_ref[...], preferred_element_type=F32) + b2_ref[0]

    _double_buffered_gather(i, nused_ref[0], h2_hbm, idx_cur, idx_nxt, (xs0, xs1), sems, compute)

    @pl.when(i >= nused_ref[0])
    def _():
        o_ref[...] = jnp.zeros_like(o_ref)


def _moe_experts(h2, slot_tok, blk_expert, blk_first, n_used, w1, b1, w2, b2):
    tm = TM_MOE
    n_slots = slot_tok.shape[0]
    nblk = n_slots // tm
    idx = slot_tok.reshape(nblk, 1, tm)
    grid_spec = pltpu.PrefetchScalarGridSpec(
        num_scalar_prefetch=3,
        grid=(nblk,),
        in_specs=[pl.BlockSpec((1, 1, tm), lambda i, be, fi, nu: (i, 0, 0), memory_space=pltpu.SMEM),
                  pl.BlockSpec((1, 1, tm), lambda i, be, fi, nu: (jnp.minimum(i + 1, nblk - 1), 0, 0),
                               memory_space=pltpu.SMEM),
                  pl.BlockSpec(memory_space=pl.ANY),
                  pl.BlockSpec((1, D_MODEL, 2 * D_EXPERT), lambda i, be, fi, nu: (be[i], 0, 0)),
                  pl.BlockSpec((1, 1, 2 * D_EXPERT), lambda i, be, fi, nu: (be[i], 0, 0)),
                  pl.BlockSpec((1, D_EXPERT, D_MODEL), lambda i, be, fi, nu: (be[i], 0, 0)),
                  pl.BlockSpec((1, 1, D_MODEL), lambda i, be, fi, nu: (be[i], 0, 0))],
        out_specs=pl.BlockSpec((tm, D_MODEL), lambda i, be, fi, nu: (i, 0)),
        scratch_shapes=[pltpu.VMEM((D_MODEL, 2 * D_EXPERT), BF16), pltpu.VMEM((D_EXPERT, D_MODEL), BF16),
                        pltpu.VMEM((tm, D_MODEL), F32), pltpu.VMEM((tm, D_MODEL), F32),
                        pltpu.SemaphoreType.DMA((2,))],
    )
    return pl.pallas_call(
        _moe_body,
        grid_spec=grid_spec,
        out_shape=jax.ShapeDtypeStruct((n_slots, D_MODEL), F32),
        compiler_params=_cparams(("arbitrary",)),
        name="moe_experts",
    )(blk_expert, blk_first, n_used, idx, idx, h2,
      w1, b1.reshape(N_EXPERTS, 1, 2 * D_EXPERT), w2, b2.reshape(N_EXPERTS, 1, D_MODEL))


def _moe(h2, top_i, rank, counts, w1, b1, w2, b2):
    tm = TM_MOE
    T = h2.shape[0]
    A = T * TOP_K
    experts = jnp.arange(N_EXPERTS, dtype=jnp.int32)
    counts = counts.reshape(N_EXPERTS).astype(jnp.int32)
    padded = ((counts + tm - 1) // tm) * tm
    pend = jnp.cumsum(padded)
    pstart = pend - padded
    n_slots = A + N_EXPERTS * tm
    nblk = n_slots // tm
    blk_lo = jnp.arange(nblk, dtype=jnp.int32) * tm
    blk_expert = jnp.minimum(jnp.sum((pend[None, :] <= blk_lo[:, None]).astype(jnp.int32), axis=1),
                             N_EXPERTS - 1).astype(jnp.int32)
    blk_first = jnp.concatenate([jnp.ones((1,), jnp.int32),
                                 (blk_expert[1:] != blk_expert[:-1]).astype(jnp.int32)])
    n_used = (pend[-1:] // tm).astype(jnp.int32)
    dest = jnp.sum(jnp.where(top_i[..., None] == experts, pstart, 0), axis=-1) + rank
    tok = jnp.arange(A, dtype=jnp.int32) // TOP_K
    slot_tok = jnp.zeros((n_slots,), jnp.int32).at[dest.reshape(-1)].set(tok, unique_indices=True)
    ys = _moe_experts(h2, slot_tok, blk_expert, blk_first, n_used, w1, b1, w2, b2)
    return ys, dest


TM_RESID = 256


def _resid_body(idx_cur, idx_nxt, x_ref, tg_ref, g_ref, np_ref, ys_hbm, o_ref, buf0, buf1, sems, *, tm):
    i = pl.program_id(0)

    def compute(buf):
        tg = tg_ref[...]
        f = buf[pl.ds(0, tm), :] * tg[:, 0:1]
        for k in range(1, TOP_K):
            f = f + buf[pl.ds(k * tm, tm), :] * tg[:, k:k + 1]
        o_ref[...] = x_ref[...] + g_ref[0] * _rms(f, np_ref[...])

    _double_buffered_gather(i, pl.num_programs(0), ys_hbm, idx_cur, idx_nxt, (buf0, buf1), sems, compute)


def _residual(x1, ys, dest, top_g, g2, npost1):
    tm = TM_RESID
    T = x1.shape[0]
    n = T // tm
    idx = jnp.transpose(dest.reshape(n, tm, TOP_K), (0, 2, 1)).reshape(n, 1, TOP_K * tm)
    tok = pl.BlockSpec((tm, D_MODEL), lambda i: (i, 0))
    return pl.pallas_call(
        functools.partial(_resid_body, tm=tm),
        grid=(n,),
        in_specs=[pl.BlockSpec((1, 1, TOP_K * tm), lambda i: (i, 0, 0), memory_space=pltpu.SMEM),
                  pl.BlockSpec((1, 1, TOP_K * tm), lambda i: (jnp.minimum(i + 1, n - 1), 0, 0),
                               memory_space=pltpu.SMEM),
                  tok, pl.BlockSpec((tm, TOP_K), lambda i: (i, 0)), _seg_spec(tm), _const_spec((1, D_MODEL)),
                  pl.BlockSpec(memory_space=pl.ANY)],
        out_specs=tok,
        out_shape=jax.ShapeDtypeStruct((T, D_MODEL), F32),
        scratch_shapes=[pltpu.VMEM((TOP_K * tm, D_MODEL), F32), pltpu.VMEM((TOP_K * tm, D_MODEL), F32),
                        pltpu.SemaphoreType.DMA((2,))],
        compiler_params=_cparams(("arbitrary",)),
        name="moe_residual",
    )(idx, idx, x1, top_g, g2, npost1, ys)


def kernel(x_prompt, x_sample, cache_k, cache_v, state_ssm, c, c_ctx, w_mod, b_mod, norm_pre, norm_post, w_in, hy_short_w, hy_short_b, hy_f_w1, hy_f_b1, hy_f_freq, hy_f_w2, hy_f_b2, hy_f_w3, hy_decay, hy_bias, attn_sink, s5_a_re, s5_a_im, s5_log_dt, s5_b_re, s5_b_im, s5_c_re, s5_c_im, s5_d, s5_glu_w, s5_glu_b, w_br_h, w_br_a, w_br_s, w_out, router_w, router_b, exp_w1, exp_b1, exp_w2, exp_b2):
    x = jnp.concatenate([x_prompt.reshape(T_CTX, D_MODEL), x_sample.reshape(T_LAT, D_MODEL)], axis=0)
    cond8 = jnp.concatenate([c_ctx[None, :], c, jnp.zeros((8 - 1 - DEC_BATCH, D_MODEL), F32)], axis=0)
    mod_all = _modulation(cond8, w_mod, b_mod)
    rope_tabs = _rope_tables()
    new_k, new_v, new_s = [], [], []
    for l in range(DEPTH):
        mods = [mod_all[l, :, j * D_MODEL:(j + 1) * D_MODEL].reshape(8, 1, D_MODEL) for j in range(6)]
        sh1, sc1, g1, sh2, sc2, g2 = mods
        w_in_b = w_in[l].astype(BF16)
        q, k, v, hy_c, s_c, hy_l, s_l = _in_proj(x, norm_pre[l, 0][None], sh1, sc1, w_in_b[:, :MIX_WIDTH],
                                                 rope_tabs)
        new_k.append(k[:T_CTX].reshape(BATCH, SEQ, N_KV_HEADS, HEAD_DIM))
        new_v.append(v[:T_CTX].reshape(BATCH, SEQ, N_KV_HEADS, HEAD_DIM))
        att = _attention(q, k, v, cache_k[:, l].reshape(DEC_BATCH, PAST_LEN, KV_WIDTH),
                         cache_v[:, l].reshape(DEC_BATCH, PAST_LEN, KV_WIDTH), attn_sink[l])
        p = dict(hy_short_w=hy_short_w[l], hy_short_b=hy_short_b[l], hy_f_w1=hy_f_w1[l], hy_f_b1=hy_f_b1[l],
                 hy_f_freq=hy_f_freq[l], hy_f_w2=hy_f_w2[l], hy_f_b2=hy_f_b2[l], hy_f_w3=hy_f_w3[l],
                 hy_decay=hy_decay[l], hy_bias=hy_bias[l], s5_a_re=s5_a_re[l], s5_a_im=s5_a_im[l],
                 s5_log_dt=s5_log_dt[l], s5_b_re=s5_b_re[l], s5_b_im=s5_b_im[l], s5_c_re=s5_c_re[l],
                 s5_c_im=s5_c_im[l])
        z_c = _hyena_group(hy_c, BATCH, p, _PLAN_CTX)
        z_l = _hyena_group(hy_l, DEC_BATCH, p, _PLAN_LAT)
        (yf_c, yb_c, yf_l, yb_l), s_fin = _s5_scan(s_c, s_l, p, state_ssm[:, l])
        new_s.append(s_fin)
        vecs = [norm_pre[l, 0][None], norm_post[l, 0][None], norm_pre[l, 1][None], s5_d[l][None]]
        x1, h2, top_i, top_g, rank, counts = _merge(
            x, att, (z_c, yf_c, yb_c, s_c), (z_l, yf_l, yb_l, s_l), [sh1, sc1, g1, sh2, sc2], vecs,
            [s5_glu_w[l].astype(BF16), s5_glu_b[l][None], w_in_b[:, MIX_WIDTH:], w_br_h[l].astype(BF16),
             w_br_a[l].astype(BF16), w_br_s[l].astype(BF16), w_out[l].astype(BF16), router_w[l],
             router_b[l][None]])
        ys, dest = _moe(h2, top_i, rank, counts, exp_w1[l], exp_b1[l], exp_w2[l], exp_b2[l])
        x = _residual(x1, ys, dest, top_g, g2, norm_post[l, 1][None])
    y_prompt = x[:T_CTX].reshape(BATCH, SEQ, D_MODEL)
    y_sample = x[T_CTX:].reshape(DEC_BATCH, DEC_SEQ, D_MODEL)
    return (y_prompt, y_sample, jnp.stack(new_k, axis=1), jnp.stack(new_v, axis=1), jnp.stack(new_s, axis=1))
```

```python
import functools
import math

import jax
import jax.numpy as jnp
import numpy as np
from jax import lax
from jax.experimental import pallas as pl
from jax.experimental.pallas import tpu as pltpu

F32 = jnp.float32
BF16 = jnp.bfloat16

D_MODEL = 1024
BATCH = 16
SEQ = 256
DEPTH = 2
DEC_BATCH = 4
DEC_SEQ = 4096
PAST_LEN = 256
GRID_W = 64
HY_WIDTH = D_MODEL // 4
HEAD_DIM = 64
N_HEADS = (D_MODEL // 2) // HEAD_DIM
N_KV_HEADS = N_HEADS // 4
GQ = N_HEADS // N_KV_HEADS
ATT_WIDTH = N_HEADS * HEAD_DIM
KV_WIDTH = N_KV_HEADS * HEAD_DIM
S5_WIDTH = D_MODEL // 4
S5_GROUP = 16
S5_GROUPS = S5_WIDTH // S5_GROUP
S5_STATE = 64
N_BRANCH = 3
HY_ORDER = 2
HY_BANDS = 16
WINDOW = 128
Q_BLOCK = 128
ROPE_BASE = 10000.0
NEG_INF = -1e30
N_EXPERTS = 32
TOP_K = 4
D_EXPERT = D_MODEL
SWIGLU_LIMIT = 7.0
SWIGLU_ALPHA = 1.702
RMS_EPS = 1e-6

T_CTX = BATCH * SEQ
T_LAT = DEC_BATCH * DEC_SEQ
T_ALL = T_CTX + T_LAT
SEG = 4096
N_SEG = T_ALL // SEG
assert T_CTX == SEG and DEC_SEQ == SEG

O_HY = 3 * HY_WIDTH
O_Q = O_HY + ATT_WIDTH
O_K = O_Q + KV_WIDTH
O_V = O_K + KV_WIDTH
O_S = O_V + S5_WIDTH
MIX_WIDTH = O_S

VMEM_LIMIT = 56 * 1024 * 1024

TM_MOE = 512


def _cparams(sem, vmem=VMEM_LIMIT):
    return pltpu.CompilerParams(dimension_semantics=sem, vmem_limit_bytes=vmem)


def _rms(x, g):
    return x * lax.rsqrt(jnp.mean(x * x, axis=-1, keepdims=True) + RMS_EPS) * g


def _mod_body(c_ref, w_ref, b_ref, o_ref):
    c = c_ref[...]
    a = c * jax.nn.sigmoid(c)
    o_ref[0] = jnp.dot(a, w_ref[0], preferred_element_type=F32,
                       precision=lax.Precision.HIGHEST) + b_ref[0]


def _modulation(cond8, w_mod, b_mod):
    tn = 1536
    n = 6 * D_MODEL
    return pl.pallas_call(
        _mod_body,
        grid=(DEPTH, n // tn),
        in_specs=[pl.BlockSpec((8, D_MODEL), lambda l, j: (0, 0)),
                  pl.BlockSpec((1, D_MODEL, tn), lambda l, j: (l, 0, j)),
                  pl.BlockSpec((1, 1, tn), lambda l, j: (l, 0, j))],
        out_specs=pl.BlockSpec((1, 8, tn), lambda l, j: (l, 0, j)),
        out_shape=jax.ShapeDtypeStruct((DEPTH, 8, n), F32),
        compiler_params=_cparams(("arbitrary", "arbitrary")),
        name="modulation",
    )(cond8, w_mod, b_mod.reshape(DEPTH, 1, n))


def _rope_tables():
    t = np.arange(SEG)
    row = (t // GRID_W).astype(np.float64)
    col = (t % GRID_W).astype(np.float64)
    quarter = HEAD_DIM // 4
    inv = ROPE_BASE ** (-np.arange(quarter, dtype=np.float64) / quarter)
    lane = np.arange(128)
    d = lane % HEAD_DIM
    pos = np.where((d // 32)[None, :] == 0, row[:, None], col[:, None])
    ang = pos * inv[d % quarter][None, :]
    first = ((d % 32) < quarter)[None, :]
    cos = np.cos(ang)
    sin = np.sin(ang)
    sin_a = np.where(first, -sin, 0.0)
    sin_b = np.where(first, 0.0, sin)
    ident = np.zeros((SEG, 128))
    tab = lambda ctx, lat: jnp.asarray(np.concatenate([ctx, lat], axis=0), F32)
    return tab(ident + 1.0, cos), tab(ident, sin_a), tab(ident, sin_b)


def _rope(x, cos, sin_a, sin_b):
    w = x.shape[-1]
    rep = w // 128
    if rep > 1:
        cos = jnp.concatenate([cos] * rep, axis=1)
        sin_a = jnp.concatenate([sin_a] * rep, axis=1)
        sin_b = jnp.concatenate([sin_b] * rep, axis=1)
    quarter = HEAD_DIM // 4
    return x * cos + pltpu.roll(x, w - quarter, 1) * sin_a + pltpu.roll(x, quarter, 1) * sin_b


TM_TOK = SEQ
N_CTX_TILES = T_CTX // TM_TOK
LAT_TILES = DEC_SEQ // TM_TOK
N_TILES = T_ALL // TM_TOK


def _ctx_tm_spec(width):
    return pl.BlockSpec((TM_TOK, width), lambda i: (0, jnp.minimum(i, N_CTX_TILES - 1)))


def _lat_tm_spec(width):
    def idx(i):
        j = jnp.maximum(i - N_CTX_TILES, 0)
        return (j % LAT_TILES, j // LAT_TILES)
    return pl.BlockSpec((TM_TOK, width), idx)


def _proj_body(x_ref, g_ref, sh_ref, sc_ref, w_ref, cos_ref, sa_ref, sb_ref,
               q_ref, k_ref, v_ref, hyc_ref, sc_out_ref, hyl_ref, sl_out_ref):
    i = pl.program_id(0)
    h = _rms(x_ref[...], g_ref[...]) * (1.0 + sc_ref[0]) + sh_ref[0]
    p = jnp.dot(h.astype(BF16), w_ref[...], preferred_element_type=F32)
    cos, sa, sb = cos_ref[...], sa_ref[...], sb_ref[...]
    q_ref[...] = _rope(p[:, O_HY:O_Q], cos, sa, sb)
    k_ref[...] = _rope(p[:, O_Q:O_K], cos, sa, sb)
    v_ref[...] = p[:, O_K:O_V]

    @pl.when(i < N_CTX_TILES)
    def _():
        hyc_ref[...] = p[:, :O_HY]
        sc_out_ref[...] = p[:, O_V:O_S]

    @pl.when(i >= N_CTX_TILES)
    def _():
        hyl_ref[...] = p[:, :O_HY]
        sl_out_ref[...] = p[:, O_V:O_S]


def _seg_spec(tm):
    return pl.BlockSpec((1, 1, D_MODEL), lambda i: ((i * tm) // SEG, 0, 0))


def _const_spec(shape):
    nd = len(shape)
    return pl.BlockSpec(shape, lambda i: (0,) * nd, pipeline_mode=pl.Buffered(1))


def _in_proj(x, gain, sh, sc, w_mix, rope_tabs):
    tm = TM_TOK
    per_seg = SEG // tm
    rope_spec = pl.BlockSpec((tm, 128), lambda i: (jnp.where(i < per_seg, i, per_seg + i % per_seg), 0))
    tok = lambda w: pl.BlockSpec((tm, w), lambda i: (i, 0))
    tok_shape = lambda w: jax.ShapeDtypeStruct((T_ALL, w), F32)
    return pl.pallas_call(
        _proj_body,
        grid=(N_TILES,),
        in_specs=[tok(D_MODEL), _const_spec((1, D_MODEL)), _seg_spec(tm), _seg_spec(tm),
                  _const_spec((D_MODEL, MIX_WIDTH)), rope_spec, rope_spec, rope_spec],
        out_specs=[tok(ATT_WIDTH), tok(KV_WIDTH), tok(KV_WIDTH),
                   _ctx_tm_spec(O_HY), _ctx_tm_spec(S5_WIDTH), _lat_tm_spec(O_HY), _lat_tm_spec(S5_WIDTH)],
        out_shape=[tok_shape(ATT_WIDTH), tok_shape(KV_WIDTH), tok_shape(KV_WIDTH),
                   jax.ShapeDtypeStruct((SEQ, BATCH * O_HY), F32),
                   jax.ShapeDtypeStruct((SEQ, BATCH * S5_WIDTH), F32),
                   jax.ShapeDtypeStruct((DEC_SEQ, DEC_BATCH * O_HY), F32),
                   jax.ShapeDtypeStruct((DEC_SEQ, DEC_BATCH * S5_WIDTH), F32)],
        compiler_params=_cparams(("arbitrary",)),
        name="in_proj",
    )(x, gain, sh, sc, w_mix, *rope_tabs)


def _attend(q, keys, vals, masks, sink_ref):
    qb = q.shape[0]
    scale = HEAD_DIM ** -0.5
    grp = lax.broadcasted_iota(jnp.int32, (GQ * qb, 1), 0) // qb
    outs = []
    for h in range(N_KV_HEADS):
        lo = h * HEAD_DIM
        qs = jnp.concatenate([q[:, (h * GQ + g) * HEAD_DIM:(h * GQ + g + 1) * HEAD_DIM]
                              for g in range(GQ)], axis=0)
        qs = (qs * scale).astype(BF16)
        sink = jnp.zeros((GQ * qb, 1), F32)
        for g in range(GQ):
            sink = jnp.where(grp == g, sink_ref[h * GQ + g], sink)
        logits = []
        for k_i, m_i in zip(keys, masks):
            s = lax.dot_general(qs, k_i[:, lo:lo + HEAD_DIM].astype(BF16),
                                (((1,), (1,)), ((), ())), preferred_element_type=F32)
            if m_i is not None:
                s = jnp.where(jnp.concatenate([m_i] * GQ, axis=0), s, NEG_INF)
            logits.append(s)
        m = sink
        for s in logits:
            m = jnp.maximum(m, jnp.max(s, axis=-1, keepdims=True))
        denom = jnp.exp(sink - m)
        acc = jnp.zeros((GQ * qb, HEAD_DIM), F32)
        for s, v_i in zip(logits, vals):
            p = jnp.exp(s - m)
            denom = denom + jnp.sum(p, axis=-1, keepdims=True)
            acc = acc + jnp.dot(p.astype(BF16), v_i[:, lo:lo + HEAD_DIM].astype(BF16),
                                preferred_element_type=F32)
        o = acc / denom
        outs.extend(o[g * qb:(g + 1) * qb] for g in range(GQ))
    return jnp.concatenate(outs, axis=1)


N_QB_CTX = T_CTX // Q_BLOCK
QB_PER_LAT = DEC_SEQ // Q_BLOCK


def _attn_body(sink_ref, q_ref, kp_ref, kc_ref, kn_ref, vp_ref, vc_ref, vn_ref, ck_ref, cv_ref, o_ref):
    g = pl.program_id(0)

    @pl.when(g < N_QB_CTX)
    def _():
        o_ref[0] = _attend(q_ref[0], [ck_ref[0]], [cv_ref[0]], [None], sink_ref)

    @pl.when(g >= N_QB_CTX)
    def _():
        i = (g - N_QB_CTX) % QB_PER_LAT
        kw = jnp.concatenate([kp_ref[0], kc_ref[0], kn_ref[0]], axis=0)
        vw = jnp.concatenate([vp_ref[0], vc_ref[0], vn_ref[0]], axis=0)
        qpos = i * Q_BLOCK + lax.broadcasted_iota(jnp.int32, (Q_BLOCK, 3 * Q_BLOCK), 0)
        kpos = (i - 1) * Q_BLOCK + lax.broadcasted_iota(jnp.int32, (Q_BLOCK, 3 * Q_BLOCK), 1)
        band = (jnp.abs(qpos - kpos) <= WINDOW) & (kpos >= 0) & (kpos < DEC_SEQ)
        o_ref[0] = _attend(q_ref[0], [kw, ck_ref[0]], [vw, cv_ref[0]], [band, None], sink_ref)


def _attention(q, k, v, ck, cv, sink):
    smem = pl.BlockSpec(memory_space=pltpu.SMEM)
    nqb = T_ALL // Q_BLOCK
    q3 = q.reshape(nqb, Q_BLOCK, ATT_WIDTH)
    k3 = k.reshape(nqb, Q_BLOCK, KV_WIDTH)
    v3 = v.reshape(nqb, Q_BLOCK, KV_WIDTH)
    ck_all = jnp.concatenate([k[:T_CTX].reshape(BATCH, SEQ, KV_WIDTH), ck], axis=0)
    cv_all = jnp.concatenate([v[:T_CTX].reshape(BATCH, SEQ, KV_WIDTH), cv], axis=0)

    def win(off):
        def idx(g):
            gl = jnp.maximum(g - N_QB_CTX, 0)
            base = N_QB_CTX + (gl // QB_PER_LAT) * QB_PER_LAT
            j = jnp.clip(gl % QB_PER_LAT + off, 0, QB_PER_LAT - 1)
            return (jnp.where(g < N_QB_CTX, g, base + j), 0, 0)
        return pl.BlockSpec((1, Q_BLOCK, KV_WIDTH), idx)

    def cache_idx(g):
        return (jnp.where(g < N_QB_CTX, g // (SEQ // Q_BLOCK), BATCH + (g - N_QB_CTX) // QB_PER_LAT), 0, 0)

    cache_spec = pl.BlockSpec((1, PAST_LEN, KV_WIDTH), cache_idx)
    qo_spec = pl.BlockSpec((1, Q_BLOCK, ATT_WIDTH), lambda g: (g, 0, 0))
    att = pl.pallas_call(
        _attn_body,
        grid=(nqb,),
        in_specs=[smem, qo_spec, win(-1), win(0), win(1), win(-1), win(0), win(1), cache_spec, cache_spec],
        out_specs=qo_spec,
        out_shape=jax.ShapeDtypeStruct((nqb, Q_BLOCK, ATT_WIDTH), F32),
        compiler_params=_cparams(("arbitrary",)),
        name="attention",
    )(sink, q3, k3, k3, k3, v3, v3, v3, ck_all, cv_all)
    return att.reshape(T_ALL, ATT_WIDTH)


def _hyena_filters(L, p):
    t = jnp.arange(L, dtype=F32)
    t_norm = t / max(L - 1, 1)
    bands = jnp.arange(1, HY_BANDS + 1, dtype=F32)
    ang = (2.0 * math.pi / L) * t[:, None] * bands[None, :]
    z = jnp.concatenate([t_norm[:, None], jnp.cos(ang), jnp.sin(ang)], axis=-1)
    h = jnp.sin(p["hy_f_freq"][0] * (z @ p["hy_f_w1"] + p["hy_f_b1"]))
    h = jnp.sin(p["hy_f_freq"][1] * (h @ p["hy_f_w2"] + p["hy_f_b2"]))
    h = (h @ p["hy_f_w3"]).astype(F32).reshape(L, HY_ORDER, 2, HY_WIDTH)
    h = h * jnp.exp(-t_norm[:, None, None, None] * jnp.abs(p["hy_decay"].astype(F32))[None])
    return h / (jnp.sum(jnp.abs(h), axis=(0, 2), keepdims=True) + 1e-6)


def _short_conv_body(x_ref, w_ref, b_ref, o_ref):
    x = x_ref[...]
    L = x.shape[0]
    t = lax.broadcasted_iota(jnp.int32, x.shape, 0)
    prev = jnp.where(t == 0, 0.0, pltpu.roll(x, 1, 0))
    nxt = jnp.where(t == L - 1, 0.0, pltpu.roll(x, L - 1, 0))
    o_ref[0] = prev * w_ref[0:1, :] + x * w_ref[1:2, :] + nxt * w_ref[2:3, :] + b_ref[...]


def _short_conv_tm(x, nb, w, b):
    L = x.shape[0]
    nct = O_HY // 128
    per = HY_WIDTH // 128
    return pl.pallas_call(
        _short_conv_body,
        grid=(nb, nct),
        in_specs=[pl.BlockSpec((L, 128), lambda bi, j: (0, bi * nct + j)),
                  pl.BlockSpec((3, 128), lambda bi, j: (0, j)),
                  pl.BlockSpec((1, 128), lambda bi, j: (0, j))],
        out_specs=pl.BlockSpec((1, L, 128), lambda bi, j: (j // per, 0, bi * per + j % per)),
        out_shape=jax.ShapeDtypeStruct((3, L, nb * HY_WIDTH), F32),
        compiler_params=_cparams(("arbitrary", "arbitrary")),
        name="hyena_short_conv",
    )(x, w, b[None])


class _FftPlan:
    def __init__(self, L, n1, n2):
        N = 2 * L
        assert n1 * n2 == N
        self.L, self.N, self.n1, self.n2 = L, N, n1, n2
        self.kin = n1 // 2
        k1 = np.arange(n1)
        th = 2.0 * np.pi * np.outer(k1, np.arange(n1)) / n1
        f_re, f_im = np.cos(th), -np.sin(th)
        self._mats = {"f_full": np.concatenate([f_re, f_im], axis=0),
                      "f_half": np.concatenate([f_re, f_im], axis=0)[:, :self.kin],
                      "f_inv": np.concatenate([f_re.T, f_im.T], axis=1)[:self.kin]}
        if n2 > 1:
            k2 = np.arange(n2)
            idx = (np.outer(k2, k2)[None] * n1 + k1[:, None, None] * k2[None, None, :]) % N
            ph = 2.0 * np.pi * idx / N
            g_re, g_im = np.cos(ph), -np.sin(ph)
            blk = lambda a, b: np.concatenate([np.concatenate([a, -b], axis=2),
                                               np.concatenate([b, a], axis=2)], axis=1)
            self._mats["g"] = blk(g_re, g_im)
            h_re, h_im = np.swapaxes(g_re, 1, 2), -np.swapaxes(g_im, 1, 2)
            self._mats["gh"] = blk(h_re, h_im)

    def __getattr__(self, name):
        mats = self.__dict__.get("_mats", {})
        if name in mats:
            return jnp.asarray(mats[name].astype(np.float32)).astype(BF16)
        raise AttributeError(name)


def _fft_s1_body(x_ref, f_ref, o_ref):
    o_ref[...] = jnp.dot(f_ref[...], x_ref[0].astype(BF16), preferred_element_type=F32).astype(o_ref.dtype)


def _fft_s1(x3, plane, f, out_dtype=BF16):
    _, kin, m = x3.shape
    rows = f.shape[0]
    tn = min(m, (4 * 1024 * 1024) // (rows * 2))
    return pl.pallas_call(
        _fft_s1_body,
        grid=(m // tn,),
        in_specs=[pl.BlockSpec((1, kin, tn), lambda j: (plane, 0, j)), pl.BlockSpec(f.shape, lambda j: (0, 0))],
        out_specs=pl.BlockSpec((rows, tn), lambda j: (0, j)),
        out_shape=jax.ShapeDtypeStruct((rows, m), out_dtype),
        compiler_params=_cparams(("arbitrary",)),
        name="fft_stage1",
    )(x3, f)


def _cmul_tiled(x_re, x_im, k_re, k_im):
    rep = x_re.shape[-1] // k_re.shape[-1]
    if rep > 1:
        k_re = jnp.concatenate([k_re] * rep, axis=-1)
        k_im = jnp.concatenate([k_im] * rep, axis=-1)
    return x_re * k_re - x_im * k_im, x_re * k_im + x_im * k_re


def _fft_mid_body(a_ref, g_ref, gh_ref, kf_ref, o_ref, *, kb, n2):
    for j in range(kb):
        a = jnp.concatenate([a_ref[0, j], a_ref[1, j]], axis=0)
        x = jnp.dot(g_ref[j], a, preferred_element_type=F32)
        y_re, y_im = _cmul_tiled(x[:n2], x[n2:], kf_ref[0, j], kf_ref[1, j])
        y = jnp.concatenate([y_re, y_im], axis=0).astype(BF16)
        b = jnp.dot(gh_ref[j], y, preferred_element_type=F32)
        o_ref[0, j] = b[:n2].astype(o_ref.dtype)
        o_ref[1, j] = b[n2:].astype(o_ref.dtype)


def _fft_fwd2_body(a_ref, g_ref, o_ref, *, kb, n2):
    for j in range(kb):
        a = jnp.concatenate([a_ref[0, j], a_ref[1, j]], axis=0)
        x = jnp.dot(g_ref[j], a, preferred_element_type=F32)
        o_ref[0, j] = x[:n2]
        o_ref[1, j] = x[n2:]


def _fft_mid(plan, a, kf=None, order=0):
    n1, n2 = plan.n1, plan.n2
    w = a.shape[1] // n2
    a4 = a.reshape(2, n1, n2, w)
    kb = 8
    a_spec = pl.BlockSpec((2, kb, n2, w), lambda i: (0, i, 0, 0))
    g_spec = pl.BlockSpec((kb, 2 * n2, 2 * n2), lambda i: (i, 0, 0))
    if kf is None:
        return pl.pallas_call(
            functools.partial(_fft_fwd2_body, kb=kb, n2=n2),
            grid=(n1 // kb,),
            in_specs=[a_spec, g_spec],
            out_specs=a_spec,
            out_shape=jax.ShapeDtypeStruct((2, n1, n2, w), F32),
            compiler_params=_cparams(("arbitrary",)),
            name="fft_stage2",
        )(a4, plan.g)
    out = pl.pallas_call(
        functools.partial(_fft_mid_body, kb=kb, n2=n2),
        grid=(n1 // kb,),
        in_specs=[a_spec, g_spec, g_spec,
                  pl.BlockSpec((2, kb, n2, HY_WIDTH), lambda i: (0, i, 0, order))],
        out_specs=a_spec,
        out_shape=jax.ShapeDtypeStruct((2, n1, n2, w), BF16),
        compiler_params=_cparams(("arbitrary",)),
        name="fft_mid",
    )(a4, plan.g, plan.gh, kf)
    return out.reshape(2 * n1, n2 * w)


def _fft_inv_body(*refs, n1, inv_n, spectral, chain):
    b_ref, fi_ref, u_ref, gate_ref, bias_ref = refs[:5]
    rest = list(refs[5:])
    bm = b_ref[...]
    if spectral:
        kf_ref = rest.pop(0)
        a = bm.astype(F32)
        y_re, y_im = _cmul_tiled(a[:n1], a[n1:], kf_ref[:n1, :], kf_ref[n1:, :])
        bm = jnp.concatenate([y_re, y_im], axis=0).astype(BF16)
    y = jnp.dot(fi_ref[...], bm, preferred_element_type=F32) * inv_n
    u = u_ref[0]
    z = gate_ref[0] * (y + u * bias_ref[...])
    if chain:
        fs_ref, z_ref, a2_ref = rest
        z_ref[0] = z
        a2_ref[...] = jnp.dot(fs_ref[...], z.astype(BF16), preferred_element_type=F32).astype(BF16)
    else:
        (z_ref,) = rest
        z_ref[0] = z


def _fft_inv(plan, bm, u3, u_plane, gate3, gate_plane, bias_t, kf=None, order=0, chain=False):
    kin, n1 = plan.kin, plan.n1
    m = bm.shape[1]
    tn = bias_t.shape[1]
    in_specs = [pl.BlockSpec((2 * n1, tn), lambda j: (0, j)),
                pl.BlockSpec((kin, 2 * n1), lambda j: (0, 0)),
                pl.BlockSpec((1, kin, tn), lambda j: (u_plane, 0, j)),
                pl.BlockSpec((1, kin, tn), lambda j: (gate_plane, 0, j)),
                pl.BlockSpec((1, tn), lambda j: (0, 0))]
    args = [bm, plan.f_inv, u3, gate3, bias_t]
    if kf is not None:
        in_specs.append(pl.BlockSpec((2 * n1, HY_WIDTH), lambda j: (0, order)))
        args.append(kf)
    out_specs = [pl.BlockSpec((1, kin, tn), lambda j: (0, 0, j))]
    out_shape = [jax.ShapeDtypeStruct((1, kin, m), F32)]
    if chain:
        in_specs.append(pl.BlockSpec((2 * n1, kin), lambda j: (0, 0)))
        args.append(plan.f_half)
        out_specs.append(pl.BlockSpec((2 * n1, tn), lambda j: (0, j)))
        out_shape.append(jax.ShapeDtypeStruct((2 * n1, m), BF16))
    return pl.pallas_call(
        functools.partial(_fft_inv_body, n1=n1, inv_n=1.0 / plan.N, spectral=kf is not None, chain=chain),
        grid=(m // tn,),
        in_specs=in_specs,
        out_specs=out_specs,
        out_shape=out_shape,
        compiler_params=_cparams(("arbitrary",)),
        name="fft_inverse",
    )(*args)


_PLAN_CTX = _FftPlan(SEQ, 2 * SEQ, 1)
_PLAN_LAT = _FftPlan(DEC_SEQ, 128, 64)


def _hyena_group(x, nb, p, plan):
    L = x.shape[0]
    w = nb * HY_WIDTH
    kin, n2 = plan.kin, plan.n2
    m = n2 * w
    sc = _short_conv_tm(x, nb, p["hy_short_w"], p["hy_short_b"]).reshape(3, kin, m)
    h = _hyena_filters(L, p)
    kk = jnp.concatenate([h[:, :, 0], jnp.zeros((1, HY_ORDER, HY_WIDTH), F32), h[:0:-1, :, 1]], axis=0)
    kk = kk.reshape(1, plan.n1, n2 * HY_ORDER * HY_WIDTH)
    kf = _fft_s1(kk, 0, plan.f_full, out_dtype=BF16 if n2 > 1 else F32)
    tn = min(m, 2048 if n2 == 1 else 8192)
    bias_t = [jnp.tile(p["hy_bias"][o], tn // HY_WIDTH)[None] for o in range(HY_ORDER)]
    a = _fft_s1(sc, 2, plan.f_half)
    if n2 > 1:
        kf = _fft_mid(plan, kf)
        bm = _fft_mid(plan, a, kf, 0)
        z, a2 = _fft_inv(plan, bm, sc, 2, sc, 0, bias_t[0], chain=True)
        bm2 = _fft_mid(plan, a2, kf, 1)
        (hy,) = _fft_inv(plan, bm2, z, 0, sc, 1, bias_t[1])
    else:
        z, a2 = _fft_inv(plan, a, sc, 2, sc, 0, bias_t[0], kf=kf, order=0, chain=True)
        (hy,) = _fft_inv(plan, a2, z, 0, sc, 1, bias_t[1], kf=kf, order=1)
    return hy.reshape(L, w)


S5_LANES = S5_GROUPS * S5_STATE
S5_CHUNK_ROWS = 1024


def _s5_spread(nb, tc):
    rows = 2 * nb
    e_f = np.zeros((tc * rows, tc * nb), np.float32)
    e_b = np.zeros((tc * rows, tc * nb), np.float32)
    t, b = np.meshgrid(np.arange(tc), np.arange(nb), indexing="ij")
    e_f[t * rows + b, t * nb + b] = 1.0
    e_b[t * rows + nb + b, (tc - 1 - t) * nb + b] = 1.0
    return e_f, e_b


def _s5_body(uf_ref, ub_ref, h0_ref, are_ref, aim_ref, ef_ref, eb_ref, eft_ref, ebt_ref, b_ref, c_ref,
             yf_ref, yb_ref, hfin_ref, s_ref, *, rows, tc):
    i = pl.program_id(0)
    dot = functools.partial(jnp.dot, preferred_element_type=F32)

    @pl.when(i == 0)
    def _():
        s_ref[pl.ds(0, rows), :] = h0_ref[...]

    u2 = jnp.concatenate([dot(ef_ref[...], uf_ref[...].astype(BF16)),
                          dot(eb_ref[...], ub_ref[...].astype(BF16))], axis=1).astype(BF16)
    s_ref[pl.ds(rows, tc * rows), :] = dot(u2, b_ref[...])
    re = pl.ds(0, S5_LANES)
    im = pl.ds(S5_LANES, S5_LANES)

    def step(t, carry):
        for r in range(rows // 8):
            src = pl.multiple_of(t * rows + r * 8, 8)
            dst = pl.multiple_of(src + rows, 8)
            a_re = are_ref[pl.ds(r * 8, 8), :]
            a_im = aim_ref[pl.ds(r * 8, 8), :]
            c_re = s_ref[pl.ds(src, 8), re]
            c_im = s_ref[pl.ds(src, 8), im]
            s_ref[pl.ds(dst, 8), re] = a_re * c_re - a_im * c_im + s_ref[pl.ds(dst, 8), re]
            s_ref[pl.ds(dst, 8), im] = a_re * c_im + a_im * c_re + s_ref[pl.ds(dst, 8), im]
        return carry

    lax.fori_loop(0, tc, step, 0)
    last = s_ref[pl.ds(tc * rows, rows), :]
    hfin_ref[...] = last
    y = dot(s_ref[pl.ds(rows, tc * rows), :].astype(BF16), c_ref[...])
    hi = y.astype(BF16)
    lo = (y - hi.astype(F32)).astype(BF16)
    yf_ref[...] = dot(eft_ref[...], hi[:, :S5_WIDTH]) + dot(eft_ref[...], lo[:, :S5_WIDTH])
    yb_ref[...] = dot(ebt_ref[...], hi[:, S5_WIDTH:]) + dot(ebt_ref[...], lo[:, S5_WIDTH:])
    s_ref[pl.ds(0, rows), :] = last


def _s5_params(p):
    a_re = jnp.minimum(p["s5_a_re"], -1e-4)
    a_im = p["s5_a_im"]
    dt = jnp.exp(p["s5_log_dt"])[..., None]
    mag = jnp.exp(a_re * dt)
    ab_re, ab_im = mag * jnp.cos(a_im * dt), mag * jnp.sin(a_im * dt)
    den = a_re * a_re + a_im * a_im
    co_re = ((ab_re - 1.0) * a_re + ab_im * a_im) / den
    co_im = (ab_im * a_re - (ab_re - 1.0) * a_im) / den
    bb_re = co_re[..., None] * p["s5_b_re"] - co_im[..., None] * p["s5_b_im"]
    bb_im = co_re[..., None] * p["s5_b_im"] + co_im[..., None] * p["s5_b_re"]
    eye = jnp.eye(S5_GROUPS, dtype=F32)
    blk_in = lambda m: jnp.einsum("dgnp,gh->dgphn", m, eye).reshape(2 * S5_WIDTH, S5_LANES)
    b_cat = jnp.concatenate([blk_in(bb_re), blk_in(bb_im)], axis=1)
    blk_out = lambda m: jnp.einsum("dgpn,gh->hndgp", m, eye).reshape(S5_LANES, 2 * S5_WIDTH)
    c_cat = jnp.concatenate([blk_out(p["s5_c_re"]), blk_out(-p["s5_c_im"])], axis=0)
    return (ab_re.reshape(2, S5_LANES), ab_im.reshape(2, S5_LANES), b_cat.astype(BF16), c_cat.astype(BF16))


def _s5_group(u_tm, nb, h0, prm):
    ab_re, ab_im, b_cat, c_cat = prm
    L = u_tm.shape[0]
    rows = 2 * nb
    tc = S5_CHUNK_ROWS // rows
    nc = L // tc
    e_f, e_b = _s5_spread(nb, tc)
    bf = lambda m: jnp.asarray(m, BF16)
    are = jnp.repeat(ab_re, nb, axis=0)
    aim = jnp.repeat(ab_im, nb, axis=0)
    const = lambda shape: pl.BlockSpec(shape, lambda i: (0, 0))
    fwd = pl.BlockSpec((tc * nb, S5_WIDTH), lambda i: (i, 0))
    bwd = pl.BlockSpec((tc * nb, S5_WIDTH), lambda i: (nc - 1 - i, 0))
    u2 = u_tm.reshape(L * nb, S5_WIDTH)
    spread, gather = (tc * rows, tc * nb), (tc * nb, tc * rows)
    y_f, y_b, hfin = pl.pallas_call(
        functools.partial(_s5_body, rows=rows, tc=tc),
        grid=(nc,),
        in_specs=[fwd, bwd, const((rows, 2 * S5_LANES)), const((rows, S5_LANES)), const((rows, S5_LANES)),
                  const(spread), const(spread), const(gather), const(gather),
                  const((2 * S5_WIDTH, 2 * S5_LANES)), const((2 * S5_LANES, 2 * S5_WIDTH))],
        out_specs=[fwd, bwd, const((rows, 2 * S5_LANES))],
        out_shape=[jax.ShapeDtypeStruct((L * nb, S5_WIDTH), F32), jax.ShapeDtypeStruct((L * nb, S5_WIDTH), F32),
                   jax.ShapeDtypeStruct((rows, 2 * S5_LANES), F32)],
        scratch_shapes=[pltpu.VMEM(((tc + 1) * rows, 2 * S5_LANES), F32)],
        compiler_params=_cparams(("arbitrary",)),
        name="s5_scan",
    )(u2, u2, h0, are, aim, bf(e_f), bf(e_b), bf(e_f.T), bf(e_b.T), b_cat, c_cat)
    return y_f.reshape(L, nb * S5_WIDTH), y_b.reshape(L, nb * S5_WIDTH), hfin


def _s5_scan(s_ctx, s_lat, p, st_lat):
    prm = _s5_params(p)
    h0_ctx = jnp.zeros((2 * BATCH, 2 * S5_LANES), F32)
    h0_lat = jnp.transpose(st_lat, (1, 0, 4, 2, 3)).reshape(2 * DEC_BATCH, 2 * S5_LANES)
    yf_c, yb_c, hfin = _s5_group(s_ctx, BATCH, h0_ctx, prm)
    yf_l, yb_l, _ = _s5_group(s_lat, DEC_BATCH, h0_lat, prm)
    new_state = jnp.transpose(hfin.reshape(2, BATCH, 2, S5_GROUPS, S5_STATE), (1, 0, 3, 4, 2))
    return (yf_c, yb_c, yf_l, yb_l), new_state


def _merge_body(x_ref, att_ref, hyc_ref, hyl_ref, yfc_ref, yfl_ref, ybc_ref, ybl_ref, sc_in_ref, sl_in_ref,
                sh1_ref, sc1_ref, g1_ref, sh2_ref, sc2_ref,
                np0_ref, npost0_ref, np1_ref, d_ref, gluw_ref, glub_ref,
                wg_ref, wh_ref, wa_ref, ws_ref, wo_ref, rw_ref, rb_ref,
                x1_ref, h2_ref, ti_ref, tg_ref, rank_ref, cnt_ref, carry_ref):
    i = pl.program_id(0)
    is_ctx = i < N_CTX_TILES
    pick = lambda c_ref, l_ref: jnp.where(is_ctx, c_ref[...], l_ref[...])
    x = x_ref[...]
    h = (_rms(x, np0_ref[...]) * (1.0 + sc1_ref[0]) + sh1_ref[0]).astype(BF16)
    dot = functools.partial(jnp.dot, preferred_element_type=F32)
    ys = pick(yfc_ref, yfl_ref) + pick(ybc_ref, ybl_ref) + pick(sc_in_ref, sl_in_ref) * d_ref[...]
    ys = jax.nn.gelu(ys)
    g = dot(ys.astype(BF16), gluw_ref[...]) + glub_ref[...]
    s = g[:, :S5_WIDTH] * jax.nn.sigmoid(g[:, S5_WIDTH:])
    hy = pick(hyc_ref, hyl_ref)
    merged = jax.nn.sigmoid(dot(h, wg_ref[:, :D_MODEL])) * dot(hy.astype(BF16), wh_ref[...])
    merged += jax.nn.sigmoid(dot(h, wg_ref[:, D_MODEL:2 * D_MODEL])) * dot(att_ref[...].astype(BF16), wa_ref[...])
    merged += jax.nn.sigmoid(dot(h, wg_ref[:, 2 * D_MODEL:])) * dot(s.astype(BF16), ws_ref[...])
    m = dot(merged.astype(BF16), wo_ref[...])
    x1 = x + g1_ref[0] * _rms(m, npost0_ref[...])
    x1_ref[...] = x1
    h2 = _rms(x1, np1_ref[...]) * (1.0 + sc2_ref[0]) + sh2_ref[0]
    h2_ref[...] = h2
    logits = jnp.dot(h2, rw_ref[...], preferred_element_type=F32,
                     precision=lax.Precision.HIGHEST) + rb_ref[...]
    lane = lax.broadcasted_iota(jnp.int32, logits.shape, 1)
    vals, idxs = [], []
    for _ in range(TOP_K):
        mx = jnp.max(logits, axis=-1, keepdims=True)
        ix = jnp.min(jnp.where(logits == mx, lane, N_EXPERTS), axis=-1, keepdims=True)
        vals.append(mx)
        idxs.append(ix)
        logits = jnp.where(lane == ix, -jnp.inf, logits)
    e = [jnp.exp(v - vals[0]) for v in vals]
    tot = e[0] + e[1] + e[2] + e[3]
    ti_ref[...] = jnp.concatenate(idxs, axis=1)
    tg_ref[...] = jnp.concatenate(e, axis=1) / tot

    @pl.when(i == 0)
    def _():
        carry_ref[...] = jnp.zeros_like(carry_ref)

    tm = x.shape[0]
    onehots = [(lane == ix).astype(F32) for ix in idxs]
    chosen = onehots[0] + onehots[1] + onehots[2] + onehots[3]
    earlier = (lax.broadcasted_iota(jnp.int32, (tm, tm), 1)
               < lax.broadcasted_iota(jnp.int32, (tm, tm), 0)).astype(BF16)
    base = carry_ref[...] + dot(earlier, chosen.astype(BF16))
    rank_ref[...] = jnp.concatenate([jnp.sum(oh * base, axis=-1, keepdims=True) for oh in onehots],
                                    axis=1).astype(jnp.int32)
    carry_ref[...] = carry_ref[...] + jnp.sum(chosen, axis=0, keepdims=True)
    cnt_ref[...] = carry_ref[...]


def _merge(x, att, seq_c, seq_l, mods, vecs, mats):
    tm = TM_TOK
    tok = lambda w: pl.BlockSpec((tm, w), lambda i: (i, 0))
    seq_specs, seq_args = [], []
    for c, l in zip(seq_c, seq_l):
        seq_specs += [_ctx_tm_spec(HY_WIDTH), _lat_tm_spec(HY_WIDTH)]
        seq_args += [c, l]
    in_specs = ([tok(D_MODEL), tok(ATT_WIDTH)] + seq_specs
                + [_seg_spec(tm)] * len(mods)
                + [_const_spec(v.shape) for v in vecs]
                + [_const_spec(m.shape) for m in mats])
    return pl.pallas_call(
        _merge_body,
        grid=(N_TILES,),
        in_specs=in_specs,
        out_specs=[tok(D_MODEL), tok(D_MODEL), tok(TOP_K), tok(TOP_K), tok(TOP_K),
                   pl.BlockSpec((1, N_EXPERTS), lambda i: (0, 0))],
        out_shape=[jax.ShapeDtypeStruct((T_ALL, D_MODEL), F32),
                   jax.ShapeDtypeStruct((T_ALL, D_MODEL), F32),
                   jax.ShapeDtypeStruct((T_ALL, TOP_K), jnp.int32),
                   jax.ShapeDtypeStruct((T_ALL, TOP_K), F32),
                   jax.ShapeDtypeStruct((T_ALL, TOP_K), jnp.int32),
                   jax.ShapeDtypeStruct((1, N_EXPERTS), F32)],
        scratch_shapes=[pltpu.VMEM((1, N_EXPERTS), F32)],
        compiler_params=_cparams(("arbitrary",)),
        name="merge_router",
    )(x, att, *seq_args, *mods, *vecs, *mats)


def _row_copies_start(src, src_row, dst, dst_row, sem, n, priority):
    for j in range(n):
        pltpu.make_async_copy(src.at[pl.ds(src_row(j), 1)], dst.at[pl.ds(dst_row(j), 1)], sem).start(
            priority=priority)


def _moe_body(be_ref, first_ref, nused_ref, idx_cur, idx_nxt, dst_prev, h2_hbm, w1_ref, b1_ref, w2_ref, b2_ref,
              ysg_hbm, w1b_ref, w2b_ref, xs0, xs1, ob0, ob1, gsem, ssem):
    i = pl.program_id(0)
    n = nused_ref[0]
    tm = xs0.shape[0]
    xs, ob = (xs0, xs1), (ob0, ob1)

    @pl.when(first_ref[i] == 1)
    def _():
        w1b_ref[0] = w1_ref[0, 0].astype(BF16)
        w2b_ref[0] = w2_ref[0, 0].astype(BF16)

    def gather_start(idx_ref, buf, sem):
        _row_copies_start(h2_hbm, lambda j: idx_ref[0, 0, j], buf, lambda j: j, sem, tm, 0)

    def gather_wait(buf, sem):
        pltpu.make_async_copy(h2_hbm.at[pl.ds(0, tm)], buf, sem).wait()

    def scatter_start(buf, sem):
        _row_copies_start(buf, lambda j: j, ysg_hbm, lambda j: dst_prev[0, 0, j], sem, tm, 1)

    def scatter_wait(buf, sem):
        pltpu.make_async_copy(buf, ysg_hbm.at[pl.ds(0, tm)], sem).wait()

    @pl.when(i == 0)
    def _():
        gather_start(idx_cur, xs0, gsem.at[0])
        ob1[...] = jnp.zeros_like(ob1)

    for par in range(2):
        @pl.when((i % 2 == par) & (i < n))
        def _(par=par):
            @pl.when(i >= 1)
            def _():
                scatter_wait(ob[par], ssem.at[par])
            gather_wait(xs[par], gsem.at[par])
            gather_start(idx_nxt, xs[1 - par], gsem.at[1 - par])
            scatter_start(ob[1 - par], ssem.at[1 - par])
            h = jnp.dot(xs[par][...].astype(BF16), w1b_ref[0], preferred_element_type=F32) + b1_ref[0, 0]
            glu = jnp.minimum(h[:, :D_EXPERT], SWIGLU_LIMIT)
            lin = jnp.clip(h[:, D_EXPERT:], -SWIGLU_LIMIT, SWIGLU_LIMIT)
            o = glu * jax.nn.sigmoid(SWIGLU_ALPHA * glu) * (lin + 1.0)
            ob[par][...] = jnp.dot(o.astype(BF16), w2b_ref[0], preferred_element_type=F32) + b2_ref[0, 0]

        @pl.when((i % 2 == par) & (i == n))
        def _(par=par):
            scatter_wait(ob[par], ssem.at[par])
            gather_wait(xs[par], gsem.at[par])
            scatter_start(ob[1 - par], ssem.at[1 - par])
            scatter_wait(ob[1 - par], ssem.at[1 - par])


def _moe_experts(h2, slot_tok, slot_dst, n_out_rows, blk_expert, blk_first, n_used, layer, w1, b1, w2, b2):
    tm = TM_MOE
    n_slots = slot_tok.shape[0]
    nblk = n_slots // tm
    idx = slot_tok.reshape(nblk, 1, tm)
    spare = n_out_rows - tm + jnp.arange(tm, dtype=jnp.int32)
    dst = jnp.concatenate([spare, slot_dst]).reshape(nblk + 1, 1, tm)
    smem = lambda f: pl.BlockSpec((1, 1, tm), lambda i, be, fi, nu: (f(i), 0, 0), memory_space=pltpu.SMEM)
    wspec = lambda r, c: pl.BlockSpec((1, 1, r, c), lambda i, be, fi, nu: (layer, be[i], 0, 0))
    grid_spec = pltpu.PrefetchScalarGridSpec(
        num_scalar_prefetch=3,
        grid=(nblk,),
        in_specs=[smem(lambda i: i), smem(lambda i: jnp.minimum(i + 1, nblk - 1)), smem(lambda i: i),
                  pl.BlockSpec(memory_space=pl.ANY),
                  wspec(D_MODEL, 2 * D_EXPERT), wspec(1, 2 * D_EXPERT), wspec(D_EXPERT, D_MODEL), wspec(1, D_MODEL)],
        out_specs=pl.BlockSpec(memory_space=pl.ANY),
        scratch_shapes=[pltpu.VMEM((1, D_MODEL, 2 * D_EXPERT), BF16), pltpu.VMEM((1, D_EXPERT, D_MODEL), BF16),
                        pltpu.VMEM((tm, D_MODEL), F32), pltpu.VMEM((tm, D_MODEL), F32),
                        pltpu.VMEM((tm, D_MODEL), F32), pltpu.VMEM((tm, D_MODEL), F32),
                        pltpu.SemaphoreType.DMA((2,)), pltpu.SemaphoreType.DMA((2,))],
    )
    return pl.pallas_call(
        _moe_body,
        grid_spec=grid_spec,
        out_shape=jax.ShapeDtypeStruct((n_out_rows, D_MODEL), F32),
        compiler_params=_cparams(("arbitrary",)),
        name="moe_experts",
    )(blk_expert, blk_first, n_used, idx, idx, dst, h2,
      w1, b1.reshape(DEPTH, N_EXPERTS, 1, 2 * D_EXPERT), w2, b2.reshape(DEPTH, N_EXPERTS, 1, D_MODEL))


def _moe(h2, top_i, rank, counts, layer, w1, b1, w2, b2):
    tm = TM_MOE
    T = h2.shape[0]
    A = T * TOP_K
    experts = jnp.arange(N_EXPERTS, dtype=jnp.int32)
    counts = counts.reshape(N_EXPERTS).astype(jnp.int32)
    padded = ((counts + tm - 1) // tm) * tm
    pend = jnp.cumsum(padded)
    pstart = pend - padded
    n_slots = A + (N_EXPERTS + 1) * tm
    nblk = n_slots // tm
    blk_lo = jnp.arange(nblk, dtype=jnp.int32) * tm
    blk_expert = jnp.minimum(jnp.sum((pend[None, :] <= blk_lo[:, None]).astype(jnp.int32), axis=1),
                             N_EXPERTS - 1).astype(jnp.int32)
    blk_first = jnp.concatenate([jnp.ones((1,), jnp.int32),
                                 (blk_expert[1:] != blk_expert[:-1]).astype(jnp.int32)])
    n_used = (pend[-1:] // tm).astype(jnp.int32)
    dest = jnp.sum(jnp.where(top_i[..., None] == experts, pstart, 0), axis=-1) + rank
    pad_dst = A + jnp.arange(n_slots, dtype=jnp.int32) % tm
    slot_dst = pad_dst.at[dest.reshape(-1)].set(jnp.arange(A, dtype=jnp.int32), unique_indices=True)
    slot_tok = jnp.where(slot_dst < A, slot_dst // TOP_K, 0)
    return _moe_experts(h2, slot_tok, slot_dst, A + tm, blk_expert, blk_first, n_used, layer, w1, b1, w2, b2)


TM_RESID = 256


def _resid_body(x_ref, ys_ref, tg_ref, g_ref, np_ref, o_ref):
    tg = tg_ref[...]
    f = ys_ref[:, :D_MODEL] * tg[:, 0:1]
    for k in range(1, TOP_K):
        f = f + ys_ref[:, k * D_MODEL:(k + 1) * D_MODEL] * tg[:, k:k + 1]
    o_ref[...] = x_ref[...] + g_ref[0] * _rms(f, np_ref[...])


def _residual(x1, ys, top_g, g2, npost1):
    tm = TM_RESID
    T = x1.shape[0]
    tok = lambda w: pl.BlockSpec((tm, w), lambda i: (i, 0))
    return pl.pallas_call(
        _resid_body,
        grid=(T // tm,),
        in_specs=[tok(D_MODEL), tok(TOP_K * D_MODEL), tok(TOP_K), _seg_spec(tm), _const_spec((1, D_MODEL))],
        out_specs=tok(D_MODEL),
        out_shape=jax.ShapeDtypeStruct((T, D_MODEL), F32),
        compiler_params=_cparams(("arbitrary",)),
        name="moe_residual",
    )(x1, ys.reshape(ys.shape[0] // TOP_K, TOP_K * D_MODEL), top_g, g2, npost1)


def kernel(x_prompt, x_sample, cache_k, cache_v, state_ssm, c, c_ctx, w_mod, b_mod, norm_pre, norm_post, w_in, hy_short_w, hy_short_b, hy_f_w1, hy_f_b1, hy_f_freq, hy_f_w2, hy_f_b2, hy_f_w3, hy_decay, hy_bias, attn_sink, s5_a_re, s5_a_im, s5_log_dt, s5_b_re, s5_b_im, s5_c_re, s5_c_im, s5_d, s5_glu_w, s5_glu_b, w_br_h, w_br_a, w_br_s, w_out, router_w, router_b, exp_w1, exp_b1, exp_w2, exp_b2):
    x = jnp.concatenate([x_prompt.reshape(T_CTX, D_MODEL), x_sample.reshape(T_LAT, D_MODEL)], axis=0)
    cond8 = jnp.concatenate([c_ctx[None, :], c, jnp.zeros((8 - 1 - DEC_BATCH, D_MODEL), F32)], axis=0)
    mod_all = _modulation(cond8, w_mod, b_mod)
    rope_tabs = _rope_tables()
    new_k, new_v, new_s = [], [], []
    for l in range(DEPTH):
        mods = [mod_all[l, :, j * D_MODEL:(j + 1) * D_MODEL].reshape(8, 1, D_MODEL) for j in range(6)]
        sh1, sc1, g1, sh2, sc2, g2 = mods
        w_in_b = w_in[l].astype(BF16)
        q, k, v, hy_c, s_c, hy_l, s_l = _in_proj(x, norm_pre[l, 0][None], sh1, sc1, w_in_b[:, :MIX_WIDTH],
                                                 rope_tabs)
        new_k.append(k[:T_CTX].reshape(BATCH, SEQ, N_KV_HEADS, HEAD_DIM))
        new_v.append(v[:T_CTX].reshape(BATCH, SEQ, N_KV_HEADS, HEAD_DIM))
        att = _attention(q, k, v, cache_k[:, l].reshape(DEC_BATCH, PAST_LEN, KV_WIDTH),
                         cache_v[:, l].reshape(DEC_BATCH, PAST_LEN, KV_WIDTH), attn_sink[l])
        p = dict(hy_short_w=hy_short_w[l], hy_short_b=hy_short_b[l], hy_f_w1=hy_f_w1[l], hy_f_b1=hy_f_b1[l],
                 hy_f_freq=hy_f_freq[l], hy_f_w2=hy_f_w2[l], hy_f_b2=hy_f_b2[l], hy_f_w3=hy_f_w3[l],
                 hy_decay=hy_decay[l], hy_bias=hy_bias[l], s5_a_re=s5_a_re[l], s5_a_im=s5_a_im[l],
                 s5_log_dt=s5_log_dt[l], s5_b_re=s5_b_re[l], s5_b_im=s5_b_im[l], s5_c_re=s5_c_re[l],
                 s5_c_im=s5_c_im[l])
        z_c = _hyena_group(hy_c, BATCH, p, _PLAN_CTX)
        z_l = _hyena_group(hy_l, DEC_BATCH, p, _PLAN_LAT)
        (yf_c, yb_c, yf_l, yb_l), s_fin = _s5_scan(s_c, s_l, p, state_ssm[:, l])
        new_s.append(s_fin)
        vecs = [norm_pre[l, 0][None], norm_post[l, 0][None], norm_pre[l, 1][None], s5_d[l][None]]
        x1, h2, top_i, top_g, rank, counts = _merge(
            x, att, (z_c, yf_c, yb_c, s_c), (z_l, yf_l, yb_l, s_l), [sh1, sc1, g1, sh2, sc2], vecs,
            [s5_glu_w[l].astype(BF16), s5_glu_b[l][None], w_in_b[:, MIX_WIDTH:], w_br_h[l].astype(BF16),
             w_br_a[l].astype(BF16), w_br_s[l].astype(BF16), w_out[l].astype(BF16), router_w[l],
             router_b[l][None]])
        ys = _moe(h2, top_i, rank, counts, l, exp_w1, exp_b1, exp_w2, exp_b2)
        x = _residual(x1, ys, top_g, g2, norm_post[l, 1][None])
    y_prompt = x[:T_CTX].reshape(BATCH, SEQ, D_MODEL)
    y_sample = x[T_CTX:].reshape(DEC_BATCH, DEC_SEQ, D_MODEL)
    return (y_prompt, y_sample, jnp.stack(new_k, axis=1), jnp.stack(new_v, axis=1), jnp.stack(new_s, axis=1))
```
